```python
import math
import jax
import jax.numpy as jnp
from jax import lax
import numpy as np

D_MODEL = 4096
BATCH = 4
SEQ = 2048
DEPTH = 2
DEC_BATCH = 32
DEC_SEQ = 1
PAST_LEN = 16384
PAGE_SIZE = 128

MIX_WIDTH = D_MODEL
HEAD_DIM = 128
ATT_WIDTH = MIX_WIDTH // 2
N_HEADS = ATT_WIDTH // HEAD_DIM
N_KV_HEADS = N_HEADS // 4
Q_PER_KV = N_HEADS // N_KV_HEADS
WINDOW = 128
ATT_BLOCK = WINDOW
ROT_DIM = HEAD_DIM // 4
ROPE_THETA = 500000.0
CONV_CH = MIX_WIDTH - ATT_WIDTH
CONV_WIDTH = 31
SSM_WIDTH = MIX_WIDTH
SSM_GROUP = 16
SSM_GROUPS = SSM_WIDTH // SSM_GROUP
SSM_STATE = 64
SSM_CHUNK = 128
D_FF = 11008
N_EVEN = (DEPTH + 1) // 2
N_ODD = DEPTH // 2
Q_COLS = N_HEADS * HEAD_DIM
KV_COLS = N_KV_HEADS * HEAD_DIM
EVEN_IN = Q_COLS + 2 * KV_COLS + 2 * CONV_CH
EPS = 1e-6
NEG_INF = -1e30

kernel_name = 'hybrid_swa_conv_s5_macaron_step'


def rms_norm(x, g):
    x32 = x.astype(jnp.float32)
    y = x32 * lax.rsqrt(jnp.mean(x32 * x32, axis=-1, keepdims=True) + EPS)
    return (y * g.astype(jnp.float32)).astype(x.dtype)


def layer_norm(x, g, b):
    x32 = x.astype(jnp.float32)
    xc = x32 - jnp.mean(x32, axis=-1, keepdims=True)
    y = xc * lax.rsqrt(jnp.mean(xc * xc, axis=-1, keepdims=True) + EPS)
    return (y * g.astype(jnp.float32) + b.astype(jnp.float32)).astype(x.dtype)


def partial_rope(x, pos):
    half = ROT_DIM // 2
    inv_freq = jnp.power(jnp.float32(ROPE_THETA), -jnp.arange(half, dtype=jnp.float32) * (2.0 / ROT_DIM))
    ang = pos.astype(jnp.float32)[:, None] * inv_freq[None, :]
    cos = jnp.cos(ang)[:, None, :]
    sin = jnp.sin(ang)[:, None, :]
    x32 = x.astype(jnp.float32)
    x1 = x32[..., :half]
    x2 = x32[..., half:ROT_DIM]
    out = jnp.concatenate([x1 * cos - x2 * sin, x2 * cos + x1 * sin, x32[..., ROT_DIM:]], axis=-1)
    return out.astype(x.dtype)


def sink_attention(q, k, v, mask, sinks):
    s = jnp.einsum('...qgrd,...sgd->...grqs', q, k, preferred_element_type=jnp.float32) * (HEAD_DIM ** -0.5)
    s = jnp.where(mask[..., None, None, :, :], s, NEG_INF)
    sink = sinks.astype(jnp.float32).reshape(N_KV_HEADS, Q_PER_KV)[:, :, None, None]
    m = jnp.maximum(jnp.max(s, axis=-1, keepdims=True), sink)
    p = jnp.exp(s - m)
    denom = jnp.sum(p, axis=-1, keepdims=True) + jnp.exp(sink - m)
    w = (p / denom).astype(v.dtype)
    return jnp.einsum('...grqs,...sgd->...qgrd', w, v)


def half_ffn(x, g, w_gu, w_down):
    h = rms_norm(x, g)
    gate, up = jnp.split(h @ w_gu, 2, axis=-1)
    return x + 0.5 * ((jax.nn.silu(gate) * up) @ w_down)


def even_mixer(h, pos, k_buf, v_buf, conv_buf, w_in, q_norm, k_norm, sinks,
               glu_b, conv_w, conv_b, ln_g, ln_b, w_out):
    b, l, _ = h.shape
    z = h @ w_in
    q = z[..., :Q_COLS].reshape(b, l, N_HEADS, HEAD_DIM)
    k = z[..., Q_COLS:Q_COLS + KV_COLS].reshape(b, l, N_KV_HEADS, HEAD_DIM)
    v = z[..., Q_COLS + KV_COLS:Q_COLS + 2 * KV_COLS].reshape(b, l, N_KV_HEADS, HEAD_DIM)
    g = z[..., Q_COLS + 2 * KV_COLS:]
    q = partial_rope(rms_norm(q, q_norm), pos).reshape(b, l, N_KV_HEADS, Q_PER_KV, HEAD_DIM)
    k = partial_rope(rms_norm(k, k_norm), pos)

    if k_buf is None:
        nb = l // ATT_BLOCK
        qb = q.reshape(b, nb, ATT_BLOCK, N_KV_HEADS, Q_PER_KV, HEAD_DIM)

        def band(t):
            tb = t.reshape(b, nb, ATT_BLOCK, N_KV_HEADS, HEAD_DIM)
            prev = jnp.concatenate([jnp.zeros_like(tb[:, :1]), tb[:, :-1]], axis=1)
            return jnp.concatenate([prev, tb], axis=2)

        q_pos = pos.reshape(nb, ATT_BLOCK)
        k_pos = q_pos[:, :1] - ATT_BLOCK + jnp.arange(2 * ATT_BLOCK, dtype=pos.dtype)[None, :]
        rel = q_pos[:, :, None] - k_pos[:, None, :]
        mask = (k_pos[:, None, :] >= 0) & (rel >= 0) & (rel <= WINDOW)
        att = sink_attention(qb, band(k), band(v), mask, sinks).reshape(b, l, ATT_WIDTH)
        keep = min(WINDOW, l)
        new_k, new_v = k[:, l - keep:], v[:, l - keep:]
    else:
        n_buf = k_buf.shape[1]
        kk = jnp.concatenate([k_buf.astype(k.dtype), k], axis=1)
        vv = jnp.concatenate([v_buf.astype(v.dtype), v], axis=1)
        k_pos = pos[0] - n_buf + jnp.arange(n_buf + l, dtype=pos.dtype)
        rel = pos[:, None] - k_pos[None, :]
        mask = (rel >= 0) & (rel <= WINDOW)
        att = sink_attention(q, kk, vv, mask, sinks).reshape(b, l, ATT_WIDTH)
        new_k, new_v = kk[:, l:], vv[:, l:]

    a, gate = jnp.split(g + glu_b, 2, axis=-1)
    gl = a * jax.nn.sigmoid(gate)
    if conv_buf is None:
        gpad = jnp.pad(gl, ((0, 0), (CONV_WIDTH - 1, 0), (0, 0)))
    else:
        gpad = jnp.concatenate([conv_buf.astype(gl.dtype), gl], axis=1)
    new_conv = gpad[:, gpad.shape[1] - (CONV_WIDTH - 1):]
    c = lax.conv_general_dilated(gpad, conv_w.astype(gl.dtype)[:, None, :], (1,), 'VALID',
                                 dimension_numbers=('NWC', 'WIO', 'NWC'),
                                 feature_group_count=CONV_CH) + conv_b
    c = jax.nn.silu(layer_norm(c, ln_g, ln_b))

    out = jnp.concatenate([att, c], axis=-1) @ w_out
    return out, new_k, new_v, new_conv


def _ssm_combine(e1, e2):
    a1, b1 = e1
    a2, b2 = e2
    return a1 * a2, a2 * b1 + b2


def odd_mixer(h, h0_re, h0_im, w_in, lam_re, lam_im, log_dt, b_re, b_im,
              c_re, c_im, d_skip, w_gate, w_out):
    b, l, _ = h.shape
    u = (h @ w_in).astype(jnp.float32).reshape(b, l, SSM_GROUPS, SSM_GROUP)
    lam = lax.complex(lam_re.astype(jnp.float32), lam_im.astype(jnp.float32))
    dt = jnp.exp(log_dt.astype(jnp.float32))[:, None]
    abar = jnp.exp(lam * dt)
    bmat = lax.complex(b_re.astype(jnp.float32), b_im.astype(jnp.float32))
    bbar = ((abar - 1.0) / lam)[:, :, None] * bmat
    cmat = lax.complex(c_re.astype(jnp.float32), c_im.astype(jnp.float32))
    dvec = d_skip.astype(jnp.float32).reshape(SSM_GROUPS, SSM_GROUP)
    if h0_re is None:
        h0 = jnp.zeros((b, SSM_GROUPS, SSM_STATE), jnp.complex64)
    else:
        h0 = lax.complex(h0_re.astype(jnp.float32), h0_im.astype(jnp.float32))
    blk = SSM_CHUNK if l % SSM_CHUNK == 0 else l
    nc = l // blk
    u_c = jnp.moveaxis(u.reshape(b, nc, blk, SSM_GROUPS, SSM_GROUP), 1, 0)

    def step(hc, ub):
        bu = jnp.einsum('blgp,gnp->blgn', ub.astype(jnp.complex64), bbar)
        a = jnp.broadcast_to(abar, bu.shape)
        a_cum, hs = lax.associative_scan(_ssm_combine, (a, bu), axis=1)
        hs = hs + a_cum * hc[:, None]
        y = jnp.real(jnp.einsum('blgn,gpn->blgp', hs, cmat)) + dvec * ub
        return hs[:, -1], y

    h_last, ys = lax.scan(step, h0, u_c)
    y = jnp.moveaxis(ys, 0, 1).reshape(b, l, SSM_WIDTH).astype(h.dtype)
    y = jax.nn.gelu(y)
    y = y * jax.nn.sigmoid(y @ w_gate)
    return y @ w_out, jnp.real(h_last), jnp.imag(h_last)


def setup_inputs(seed: int = 0) -> dict:
    key = jax.random.key(seed)
    keys = iter(jax.random.split(key, 40))
    f32 = jnp.float32

    def normal(shape, scale):
        return jax.random.normal(next(keys), shape, f32) * scale

    win_buf = min(WINDOW, PAST_LEN)
    state_idx = jnp.arange(SSM_STATE, dtype=f32)
    return {
        'x_prompt': normal((BATCH, SEQ, D_MODEL), 1.0),
        'x_sample': normal((DEC_BATCH, DEC_SEQ, D_MODEL), 1.0),
        'cache_swa_k': normal((N_EVEN, DEC_BATCH, win_buf, N_KV_HEADS, HEAD_DIM), 1.0),
        'cache_swa_v': normal((N_EVEN, DEC_BATCH, win_buf, N_KV_HEADS, HEAD_DIM), 1.0),
        'state_conv': normal((N_EVEN, DEC_BATCH, CONV_WIDTH - 1, CONV_CH), 0.5),
        'state_ssm_re': normal((N_ODD, DEC_BATCH, SSM_GROUPS, SSM_STATE), 0.2),
        'state_ssm_im': normal((N_ODD, DEC_BATCH, SSM_GROUPS, SSM_STATE), 0.2),
        'ffn_norm': 1.0 + normal((DEPTH, 2, D_MODEL), 0.02),
        'ffn_w_gu': normal((DEPTH, 2, D_MODEL, 2 * D_FF), D_MODEL ** -0.5),
        'ffn_w_down': normal((DEPTH, 2, D_FF, D_MODEL), D_FF ** -0.5),
        'mix_norm': 1.0 + normal((DEPTH, D_MODEL), 0.02),
        'even_w_in': normal((N_EVEN, D_MODEL, EVEN_IN), D_MODEL ** -0.5),
        'even_q_norm': 1.0 + normal((N_EVEN, HEAD_DIM), 0.02),
        'even_k_norm': 1.0 + normal((N_EVEN, HEAD_DIM), 0.02),
        'even_sinks': normal((N_EVEN, N_HEADS), 0.5),
        'even_glu_b': normal((N_EVEN, 2 * CONV_CH), 0.01),
        'even_conv_w': normal((N_EVEN, CONV_WIDTH, CONV_CH), CONV_WIDTH ** -0.5),
        'even_conv_b': normal((N_EVEN, CONV_CH), 0.01),
        'even_ln_g': 1.0 + normal((N_EVEN, CONV_CH), 0.02),
        'even_ln_b': normal((N_EVEN, CONV_CH), 0.01),
        'even_w_out': normal((N_EVEN, MIX_WIDTH, D_MODEL), MIX_WIDTH ** -0.5),
        'odd_w_in': normal((N_ODD, D_MODEL, SSM_WIDTH), D_MODEL ** -0.5),
        'odd_lam_re': -0.5 + normal((N_ODD, SSM_GROUPS, SSM_STATE), 0.01),
        'odd_lam_im': math.pi * state_idx + normal((N_ODD, SSM_GROUPS, SSM_STATE), 0.01),
        'odd_log_dt': jax.random.uniform(next(keys), (N_ODD, SSM_GROUPS), f32,
                                         math.log(1e-3), math.log(1e-1)),
        'odd_b_re': normal((N_ODD, SSM_GROUPS, SSM_STATE, SSM_GROUP), SSM_GROUP ** -0.5),
        'odd_b_im': normal((N_ODD, SSM_GROUPS, SSM_STATE, SSM_GROUP), SSM_GROUP ** -0.5),
        'odd_c_re': normal((N_ODD, SSM_GROUPS, SSM_GROUP, SSM_STATE), SSM_STATE ** -0.5),
        'odd_c_im': normal((N_ODD, SSM_GROUPS, SSM_GROUP, SSM_STATE), SSM_STATE ** -0.5),
        'odd_d': normal((N_ODD, SSM_WIDTH), 0.5),
        'odd_w_gate': normal((N_ODD, SSM_WIDTH, SSM_WIDTH), SSM_WIDTH ** -0.5),
        'odd_w_out': normal((N_ODD, SSM_WIDTH, D_MODEL), SSM_WIDTH ** -0.5),
    }


def reference(x_prompt, x_sample, cache_swa_k, cache_swa_v, state_conv, state_ssm_re, state_ssm_im,
              ffn_norm, ffn_w_gu, ffn_w_down, mix_norm,
              even_w_in, even_q_norm, even_k_norm, even_sinks, even_glu_b, even_conv_w, even_conv_b,
              even_ln_g, even_ln_b, even_w_out,
              odd_w_in, odd_lam_re, odd_lam_im, odd_log_dt, odd_b_re, odd_b_im, odd_c_re, odd_c_im,
              odd_d, odd_w_gate, odd_w_out):

    def run(x, pos, cached):
        ks, vs, cs, res, ims = [], [], [], [], []
        for layer in range(DEPTH):
            i = layer // 2
            x = half_ffn(x, ffn_norm[layer, 0], ffn_w_gu[layer, 0], ffn_w_down[layer, 0])
            h = rms_norm(x, mix_norm[layer])
            if layer % 2 == 0:
                out, nk, nv, ncv = even_mixer(
                    h, pos,
                    cache_swa_k[i] if cached else None,
                    cache_swa_v[i] if cached else None,
                    state_conv[i] if cached else None,
                    even_w_in[i], even_q_norm[i], even_k_norm[i], even_sinks[i], even_glu_b[i],
                    even_conv_w[i], even_conv_b[i], even_ln_g[i], even_ln_b[i], even_w_out[i])
                ks.append(nk)
                vs.append(nv)
                cs.append(ncv)
            else:
                out, hre, him = odd_mixer(
                    h,
                    state_ssm_re[i] if cached else None,
                    state_ssm_im[i] if cached else None,
                    odd_w_in[i], odd_lam_re[i], odd_lam_im[i], odd_log_dt[i], odd_b_re[i], odd_b_im[i],
                    odd_c_re[i], odd_c_im[i], odd_d[i], odd_w_gate[i], odd_w_out[i])
                res.append(hre)
                ims.append(him)
            x = x + out
            x = half_ffn(x, ffn_norm[layer, 1], ffn_w_gu[layer, 1], ffn_w_down[layer, 1])
        return x, jnp.stack(ks), jnp.stack(vs), jnp.stack(cs), jnp.stack(res), jnp.stack(ims)

    pos_p = jnp.arange(x_prompt.shape[1], dtype=jnp.int32)
    pos_s = PAST_LEN + jnp.arange(x_sample.shape[1], dtype=jnp.int32)
    y_prompt, pk, pv, pc, pre, pim = run(x_prompt, pos_p, False)
    y_sample, sk, sv, sc, sre, sim = run(x_sample, pos_s, True)
    return (y_prompt, y_sample, pk, pv, pc, pre, pim, sk, sv, sc, sre, sim)
```

```python
import functools
import math

import jax
import jax.numpy as jnp
from jax import lax
from jax.experimental import pallas as pl
from jax.experimental.pallas import tpu as pltpu

F32 = jnp.float32
BF16 = jnp.bfloat16

D_MODEL = 4096
BATCH = 4
SEQ = 2048
DEC_BATCH = 32
PAST_LEN = 16384
HEAD_DIM = 128
N_HEADS = 16
N_KV_HEADS = 4
Q_PER_KV = 4
WINDOW = 128
ROT_DIM = 32
ROPE_THETA = 500000.0
ATT_WIDTH = 2048
CONV_CH = 2048
CONV_WIDTH = 31
Q_COLS = 2048
KV_COLS = 512
EVEN_IN = 7168
SSM_GROUPS = 256
SSM_GROUP = 16
SSM_STATE = 64
D_FF = 11008
EPS = 1e-6
NEG_INF = -1e30

V7X_VMEM_BYTES = 64 * 1024 * 1024
V7X_LANES = 128
V7X_SUBLANES = 8
VMEM_CAP = V7X_VMEM_BYTES - 6 * 1024 * 1024

SSM_TILE_GROUPS = 16
SSM_TILE_CH = SSM_TILE_GROUPS * SSM_GROUP
SSM_TILE_ST = SSM_TILE_GROUPS * SSM_STATE
SSM_TILES = SSM_GROUPS // SSM_TILE_GROUPS
CONV_HALO = 32


def _cparams(n_grid, est_bytes):
    limit = int(min(VMEM_CAP, est_bytes * 5 // 4 + (4 << 20)))
    return pltpu.CompilerParams(dimension_semantics=("arbitrary",) * n_grid,
                                vmem_limit_bytes=limit)


def _sigmoid(x):
    return 1.0 / (1.0 + jnp.exp(-x))


def _rmsnorm_body(x_ref, g_ref, o_ref):
    x = x_ref[...]
    ms = jnp.mean(x * x, axis=-1, keepdims=True)
    o_ref[...] = (x * lax.rsqrt(ms + EPS) * g_ref[...]).astype(o_ref.dtype)


def _rmsnorm(x, g):
    m, d = x.shape
    tr = min(m, 256)
    return pl.pallas_call(
        _rmsnorm_body,
        out_shape=jax.ShapeDtypeStruct((m, d), BF16),
        grid=(m // tr,),
        in_specs=[pl.BlockSpec((tr, d), lambda i: (i, 0)),
                  pl.BlockSpec((1, d), lambda i: (0, 0))],
        out_specs=pl.BlockSpec((tr, d), lambda i: (i, 0)),
        compiler_params=_cparams(1, 2 * tr * d * 4 + 2 * tr * d * 2),
        name="rmsnorm",
    )(x, g.reshape(1, d))


def _mm_body(*refs, n_w, epilogue, scale):
    a_ref = refs[0]
    w_refs = refs[1:1 + n_w]
    pos = 1 + n_w
    extra_ref = None
    if epilogue in ("res", "gate"):
        extra_ref = refs[pos]
        pos += 1
    o_ref = refs[pos]
    wb_refs = refs[pos + 1:pos + 1 + n_w]

    @pl.when(pl.program_id(1) == 0)
    def _cast_weights():
        for w_ref, wb_ref in zip(w_refs, wb_refs):
            wb_ref[...] = w_ref[...].astype(BF16)

    a = a_ref[...].astype(BF16)
    accs = [jnp.dot(a, wb_ref[...], preferred_element_type=F32) for wb_ref in wb_refs]
    if epilogue == "plain":
        out = accs[0]
    elif epilogue == "swiglu":
        gate, up = accs
        out = gate * _sigmoid(gate) * up
    elif epilogue == "res":
        out = extra_ref[...] + scale * accs[0]
    elif epilogue == "gate":
        out = extra_ref[...].astype(F32) * _sigmoid(accs[0])
    o_ref[...] = out.astype(o_ref.dtype)


def _matmul(a, w, widx, *, n_cols, tm, tn, epilogue="plain", out_dtype=F32, col_offs=(0,),
            extra=None, scale=1.0, a_tmajor=False, extra_tmajor=False, out_tmajor=False, name="mm"):
    m, k = a.shape
    n_w = len(col_offs)
    nt = SEQ // tm if (a_tmajor or extra_tmajor or out_tmajor) else 1
    nlead = len(widx)

    def tmajor_map(ncb):
        return lambda j, i: (i % nt, (i // nt) * ncb + j)

    if a_tmajor:
        a_in = a.reshape(SEQ, BATCH * k)
        a_spec = pl.BlockSpec((tm, k), lambda j, i: (i % nt, i // nt))
    else:
        a_in = a
        a_spec = pl.BlockSpec((tm, k), lambda j, i: (i, 0))
    operands = [a_in]
    in_specs = [a_spec]
    for off in col_offs:
        operands.append(w)
        in_specs.append(pl.BlockSpec((None,) * nlead + (k, tn),
                                     lambda j, i, off=off: tuple(widx) + (0, j + off)))
    est = 2 * tm * k * a.dtype.itemsize + n_w * (2 * k * tn * 4 + k * tn * 2)
    if extra is not None:
        if extra_tmajor:
            operands.append(extra.reshape(SEQ, BATCH * n_cols))
            in_specs.append(pl.BlockSpec((tm, tn), tmajor_map(n_cols // tn)))
        else:
            operands.append(extra)
            in_specs.append(pl.BlockSpec((tm, tn), lambda j, i: (i, j)))
        est += 2 * tm * tn * extra.dtype.itemsize
    if out_tmajor:
        out_shape = jax.ShapeDtypeStruct((SEQ, BATCH * n_cols), out_dtype)
        out_spec = pl.BlockSpec((tm, tn), tmajor_map(n_cols // tn))
    else:
        out_shape = jax.ShapeDtypeStruct((m, n_cols), out_dtype)
        out_spec = pl.BlockSpec((tm, tn), lambda j, i: (i, j))
    est += 2 * tm * tn * jnp.dtype(out_dtype).itemsize + (n_w + 1) * tm * tn * 4

    out = pl.pallas_call(
        functools.partial(_mm_body, n_w=n_w, epilogue=epilogue, scale=scale),
        out_shape=out_shape,
        grid=(n_cols // tn, m // tm),
        in_specs=in_specs,
        out_specs=out_spec,
        scratch_shapes=[pltpu.VMEM((k, tn), BF16) for _ in range(n_w)],
        compiler_params=_cparams(2, est),
        name=name,
    )(*operands)
    return out.reshape(m, n_cols)


def _half_ffn(x, g, w_gu, w_down, widx, *, tm, tm_down, out_tmajor=False, in_tmajor=False):
    h = _rmsnorm(x, g)
    act = _matmul(h, w_gu, widx, n_cols=D_FF, tm=tm, tn=256, epilogue="swiglu", out_dtype=BF16,
                  col_offs=(0, D_FF // 256), name="ffn_gu")
    return _matmul(act, w_down, widx, n_cols=D_MODEL, tm=tm_down, tn=256, epilogue="res",
                   extra=x, scale=0.5, a_tmajor=in_tmajor, extra_tmajor=in_tmajor,
                   out_tmajor=out_tmajor, name="ffn_down")


def _rope_tables(pos):
    half = ROT_DIM // 2
    inv_freq = jnp.power(jnp.float32(ROPE_THETA), -jnp.arange(half, dtype=F32) * (2.0 / ROT_DIM))
    ang = pos.astype(F32)[:, None] * inv_freq[None, :]
    cos, sin = jnp.cos(ang), jnp.sin(ang)
    n = pos.shape[0]
    rest = HEAD_DIM - ROT_DIM
    c = jnp.concatenate([cos, cos, jnp.ones((n, rest), F32)], axis=1)
    sp = jnp.concatenate([jnp.zeros((n, half), F32), sin, jnp.zeros((n, rest), F32)], axis=1)
    sm = jnp.concatenate([-sin, jnp.zeros((n, HEAD_DIM - half), F32)], axis=1)
    return c, sp, sm


def _norm_rope(x, g, c, sp, sm):
    half = ROT_DIM // 2
    y = x * lax.rsqrt(jnp.mean(x * x, axis=-1, keepdims=True) + EPS) * g
    return y * c + pltpu.roll(y, half, 1) * sp + pltpu.roll(y, HEAD_DIM - half, 1) * sm


def _layer_norm_silu(c, g, b):
    mu = jnp.mean(c, axis=-1, keepdims=True)
    xc = c - mu
    var = jnp.mean(xc * xc, axis=-1, keepdims=True)
    y = xc * lax.rsqrt(var + EPS) * g + b
    return y * _sigmoid(y)


def _even_prompt_body(sinks_ref, q_ref, kv_ref, a0_ref, a1_ref, g0_ref, g1_ref,
                      c_ref, sp_ref, sm_ref, qn_ref, kn_ref, glub_ref, cw_ref, cb_ref,
                      lng_ref, lnb_ref,
                      mix_ref, nk_ref, nv_ref, nc_ref,
                      kprev_ref, vprev_ref, gbuf_ref, cbuf_ref):
    blk = WINDOW
    i = pl.program_id(1)

    @pl.when(i == 0)
    def _reset():
        kprev_ref[...] = jnp.zeros_like(kprev_ref)
        vprev_ref[...] = jnp.zeros_like(vprev_ref)
        gbuf_ref[0:CONV_HALO, :] = jnp.zeros((CONV_HALO, CONV_CH), F32)

    c, sp, sm = c_ref[...], sp_ref[...], sm_ref[...]
    qn, kn = qn_ref[...], kn_ref[...]

    row = lax.broadcasted_iota(jnp.int32, (blk, 2 * blk), 0)
    col = lax.broadcasted_iota(jnp.int32, (blk, 2 * blk), 1)
    first_key = jnp.where(i > 0, 0, blk)
    mask = (col >= jnp.maximum(row, first_key)) & (col <= row + WINDOW)
    scale = HEAD_DIM ** -0.5
    for g in range(N_KV_HEADS):
        ks = slice(g * HEAD_DIM, (g + 1) * HEAD_DIM)
        k_cur = _norm_rope(kv_ref[:, ks], kn, c, sp, sm)
        v_cur = kv_ref[:, KV_COLS + g * HEAD_DIM:KV_COLS + (g + 1) * HEAD_DIM]
        nk_ref[0, :, ks] = k_cur
        nv_ref[0, :, ks] = v_cur
        k_cur_b = k_cur.astype(BF16)
        v_cur_b = v_cur.astype(BF16)
        k2 = jnp.concatenate([kprev_ref[:, ks], k_cur_b], axis=0)
        v2 = jnp.concatenate([vprev_ref[:, ks], v_cur_b], axis=0)
        for r in range(Q_PER_KV):
            h = g * Q_PER_KV + r
            hs = slice(h * HEAD_DIM, (h + 1) * HEAD_DIM)
            qh = _norm_rope(q_ref[:, hs], qn, c, sp, sm).astype(BF16)
            s = lax.dot_general(qh, k2, (((1,), (1,)), ((), ())),
                                preferred_element_type=F32) * scale
            s = jnp.where(mask, s, NEG_INF)
            sink = sinks_ref[h]
            m = jnp.maximum(jnp.max(s, axis=-1, keepdims=True), sink)
            p = jnp.exp(s - m)
            denom = jnp.sum(p, axis=-1, keepdims=True) + jnp.exp(sink - m)
            w = (p / denom).astype(BF16)
            o = jnp.dot(w, v2, preferred_element_type=F32)
            mix_ref[:, hs] = o.astype(mix_ref.dtype)
        kprev_ref[:, ks] = k_cur_b
        vprev_ref[:, ks] = v_cur_b

    half_ch = CONV_CH // 2
    for hh, (a_ref, g_ref) in enumerate(((a0_ref, g0_ref), (a1_ref, g1_ref))):
        cs = slice(hh * half_ch, (hh + 1) * half_ch)
        a = a_ref[...] + glub_ref[:, cs]
        gate = g_ref[...] + glub_ref[:, CONV_CH + hh * half_ch:CONV_CH + (hh + 1) * half_ch]
        gbuf_ref[CONV_HALO:CONV_HALO + blk, cs] = a * _sigmoid(gate)

    lane_chunk = 2 * V7X_LANES
    first = CONV_HALO - (CONV_WIDTH - 1)

    def conv_chunk(ci, carry):
        ls = pl.ds(pl.multiple_of(ci * lane_chunk, lane_chunk), lane_chunk)
        acc = jnp.zeros((blk, lane_chunk), F32)
        for w in range(CONV_WIDTH):
            acc = acc + cw_ref[w:w + 1, ls] * gbuf_ref[first + w:first + w + blk, ls]
        cbuf_ref[:, ls] = acc
        return carry

    lax.fori_loop(0, CONV_CH // lane_chunk, conv_chunk, 0)
    y = _layer_norm_silu(cbuf_ref[...] + cb_ref[...], lng_ref[...], lnb_ref[...])
    mix_ref[:, ATT_WIDTH:ATT_WIDTH + CONV_CH] = y.astype(mix_ref.dtype)

    @pl.when(i == pl.num_programs(1) - 1)
    def _emit_conv_state():
        nc_ref[0] = gbuf_ref[CONV_HALO + blk - (CONV_WIDTH - 1):CONV_HALO + blk, :]

    gbuf_ref[0:CONV_HALO, :] = gbuf_ref[blk:blk + CONV_HALO, :]


def _even_core_prompt(z, sinks, qn, kn, glub, cw, cb, lng, lnb):
    nb = SEQ // WINDOW
    blk = WINDOW
    c, sp, sm = _rope_tables(jnp.arange(SEQ, dtype=jnp.int32))
    rowmap = lambda col: (lambda b, i: (b * nb + i, col))
    tab = pl.BlockSpec((blk, HEAD_DIM), lambda b, i: (i, 0))
    vec = lambda n: pl.BlockSpec((1, n), lambda b, i: (0, 0))
    est = (2 * blk * (2048 + 1024 + 4 * 1024) * 4 + 2 * blk * 4096 * 2 + 4 * blk * 512 * 4
           + (blk + CONV_HALO) * CONV_CH * 4 + 3 * blk * CONV_CH * 4 + 2 * CONV_WIDTH * CONV_CH * 4 + (8 << 20))
    return pl.pallas_call(
        _even_prompt_body,
        out_shape=(jax.ShapeDtypeStruct((BATCH * SEQ, 2 * ATT_WIDTH), BF16),
                   jax.ShapeDtypeStruct((BATCH, blk, KV_COLS), F32),
                   jax.ShapeDtypeStruct((BATCH, blk, KV_COLS), F32),
                   jax.ShapeDtypeStruct((BATCH, CONV_WIDTH - 1, CONV_CH), F32)),
        grid=(BATCH, nb),
        in_specs=[pl.BlockSpec(memory_space=pltpu.SMEM),
                  pl.BlockSpec((blk, 2048), rowmap(0)),
                  pl.BlockSpec((blk, 1024), rowmap(2)),
                  pl.BlockSpec((blk, 1024), rowmap(3)),
                  pl.BlockSpec((blk, 1024), rowmap(4)),
                  pl.BlockSpec((blk, 1024), rowmap(5)),
                  pl.BlockSpec((blk, 1024), rowmap(6)),
                  tab, tab, tab, vec(HEAD_DIM), vec(HEAD_DIM), vec(2 * CONV_CH),
                  pl.BlockSpec((CONV_WIDTH, CONV_CH), lambda b, i: (0, 0)),
                  vec(CONV_CH), vec(CONV_CH), vec(CONV_CH)],
        out_specs=(pl.BlockSpec((blk, 2 * ATT_WIDTH), rowmap(0)),
                   pl.BlockSpec((1, blk, KV_COLS), lambda b, i: (b, 0, 0)),
                   pl.BlockSpec((1, blk, KV_COLS), lambda b, i: (b, 0, 0)),
                   pl.BlockSpec((1, CONV_WIDTH - 1, CONV_CH), lambda b, i: (b, 0, 0))),
        scratch_shapes=[pltpu.VMEM((blk, KV_COLS), BF16), pltpu.VMEM((blk, KV_COLS), BF16),
                        pltpu.VMEM((blk + CONV_HALO, CONV_CH), F32), pltpu.VMEM((blk, CONV_CH), F32)],
        compiler_params=_cparams(2, est),
        name="even_core_prompt",
    )(sinks, z, z, z, z, z, z, c, sp, sm, qn.reshape(1, -1), kn.reshape(1, -1), glub.reshape(1, -1),
      cw, cb.reshape(1, -1), lng.reshape(1, -1), lnb.reshape(1, -1))


def _even_sample_body(sinks_ref, z_ref, ck_ref, cv_ref, sc_ref, c_ref, sp_ref, sm_ref, qn_ref, kn_ref,
                      glub_ref, cw_ref, cb_ref, lng_ref, lnb_ref,
                      mix_ref, nk_ref, nv_ref, nc_ref):
    c, sp, sm = c_ref[...], sp_ref[...], sm_ref[...]
    qn, kn = qn_ref[...], kn_ref[...]
    nbuf = WINDOW
    scale = HEAD_DIM ** -0.5
    qrow = lax.broadcasted_iota(jnp.int32, (V7X_SUBLANES, 1), 0)
    krow = lax.broadcasted_iota(jnp.int32, (nbuf, HEAD_DIM), 0)
    for g in range(N_KV_HEADS):
        ks = slice(g * HEAD_DIM, (g + 1) * HEAD_DIM)
        k_new = _norm_rope(z_ref[0, :, Q_COLS + g * HEAD_DIM:Q_COLS + (g + 1) * HEAD_DIM], kn, c, sp, sm)
        v_new = z_ref[0, :, Q_COLS + KV_COLS + g * HEAD_DIM:Q_COLS + KV_COLS + (g + 1) * HEAD_DIM]
        k_old = ck_ref[0, :, ks]
        v_old = cv_ref[0, :, ks]
        nk_ref[0, :, ks] = jnp.where(krow == nbuf - 1, k_new, pltpu.roll(k_old, nbuf - 1, 0))
        nv_ref[0, :, ks] = jnp.where(krow == nbuf - 1, v_new, pltpu.roll(v_old, nbuf - 1, 0))
        qs = [_norm_rope(z_ref[0, :, (g * Q_PER_KV + r) * HEAD_DIM:(g * Q_PER_KV + r + 1) * HEAD_DIM],
                         qn, c, sp, sm) for r in range(Q_PER_KV)]
        q8 = jnp.zeros((V7X_SUBLANES, HEAD_DIM), F32)
        for r in range(Q_PER_KV):
            q8 = jnp.where(qrow == r, qs[r], q8)
        q8b = q8.astype(BF16)
        s = lax.dot_general(q8b, k_old.astype(BF16), (((1,), (1,)), ((), ())),
                            preferred_element_type=F32) * scale
        k_new_r = k_new.astype(BF16).astype(F32)
        v_new_r = v_new.astype(BF16).astype(F32)
        s_self = jnp.sum(q8b.astype(F32) * k_new_r, axis=-1, keepdims=True) * scale
        sink = jnp.zeros((V7X_SUBLANES, 1), F32)
        for r in range(Q_PER_KV):
            sink = jnp.where(qrow == r, sinks_ref[g * Q_PER_KV + r], sink)
        m = jnp.maximum(jnp.maximum(jnp.max(s, axis=-1, keepdims=True), s_self), sink)
        p = jnp.exp(s - m)
        p_self = jnp.exp(s_self - m)
        denom = jnp.sum(p, axis=-1, keepdims=True) + p_self + jnp.exp(sink - m)
        o = jnp.dot((p / denom).astype(BF16), v_old.astype(BF16), preferred_element_type=F32)
        o = o + (p_self / denom).astype(BF16).astype(F32) * v_new_r
        for r in range(Q_PER_KV):
            h = g * Q_PER_KV + r
            mix_ref[0, :, h * HEAD_DIM:(h + 1) * HEAD_DIM] = o[r:r + 1, :]

    a = z_ref[0, :, Q_COLS + 2 * KV_COLS:Q_COLS + 2 * KV_COLS + CONV_CH] + glub_ref[:, 0:CONV_CH]
    gate = z_ref[0, :, Q_COLS + 2 * KV_COLS + CONV_CH:EVEN_IN] + glub_ref[:, CONV_CH:2 * CONV_CH]
    gl = a * _sigmoid(gate)
    nst = CONV_WIDTH - 1
    conv = jnp.sum(cw_ref[0:nst, :] * sc_ref[0], axis=0, keepdims=True) + cw_ref[nst:nst + 1, :] * gl
    y = _layer_norm_silu(conv + cb_ref[...], lng_ref[...], lnb_ref[...])
    mix_ref[0, :, ATT_WIDTH:ATT_WIDTH + CONV_CH] = y
    nc_ref[0, 0:nst - 1, :] = sc_ref[0, 1:nst, :]
    nc_ref[0, nst - 1:nst, :] = gl


def _even_core_sample(z, cache_k, cache_v, state_conv, sinks, qn, kn, glub, cw, cb, lng, lnb):
    nb = DEC_BATCH
    c, sp, sm = _rope_tables(jnp.full((1,), PAST_LEN, jnp.int32))
    vec = lambda n: pl.BlockSpec((1, n), lambda b: (0, 0))
    per_b = lambda r, n: pl.BlockSpec((1, r, n), lambda b: (b, 0, 0))
    nst = CONV_WIDTH - 1
    est = 2 * (EVEN_IN * 4 * 8 + 4 * WINDOW * KV_COLS * 4 + 2 * 32 * CONV_CH * 4 + 4096 * 4 * 8) + (8 << 20)
    return pl.pallas_call(
        _even_sample_body,
        out_shape=(jax.ShapeDtypeStruct((nb, 1, 2 * ATT_WIDTH), F32),
                   jax.ShapeDtypeStruct((nb, WINDOW, KV_COLS), F32),
                   jax.ShapeDtypeStruct((nb, WINDOW, KV_COLS), F32),
                   jax.ShapeDtypeStruct((nb, nst, CONV_CH), F32)),
        grid=(nb,),
        in_specs=[pl.BlockSpec(memory_space=pltpu.SMEM),
                  per_b(1, EVEN_IN), per_b(WINDOW, KV_COLS), per_b(WINDOW, KV_COLS), per_b(nst, CONV_CH),
                  vec(HEAD_DIM), vec(HEAD_DIM), vec(HEAD_DIM), vec(HEAD_DIM), vec(HEAD_DIM),
                  vec(2 * CONV_CH), pl.BlockSpec((CONV_WIDTH, CONV_CH), lambda b: (0, 0)),
                  vec(CONV_CH), vec(CONV_CH), vec(CONV_CH)],
        out_specs=(per_b(1, 2 * ATT_WIDTH), per_b(WINDOW, KV_COLS), per_b(WINDOW, KV_COLS),
                   per_b(nst, CONV_CH)),
        compiler_params=_cparams(1, est),
        name="even_core_sample",
    )(sinks, z.reshape(nb, 1, EVEN_IN), cache_k, cache_v, state_conv, c, sp, sm,
      qn.reshape(1, -1), kn.reshape(1, -1), glub.reshape(1, -1), cw, cb.reshape(1, -1),
      lng.reshape(1, -1), lnb.reshape(1, -1))


def _ssm_prep_body(lre_ref, lim_ref, ldt_ref, btre_ref, btim_ref,
                   are_ref, aim_ref, bbre_ref, bbim_ref, cre_ref, cim_ref):
    lre, lim = lre_ref[...], lim_ref[...]
    dt = jnp.exp(ldt_ref[...])
    mag = jnp.exp(lre * dt)
    ang = lim * dt
    are = mag * jnp.cos(ang)
    aim = mag * jnp.sin(ang)
    are_ref[...] = are
    aim_ref[...] = aim
    nre, nim = are - 1.0, aim
    den = lre * lre + lim * lim
    cre_ref[...] = (nre * lre + nim * lim) / den
    cim_ref[...] = (nim * lre - nre * lim) / den

    def per_group(g, carry):
        cr = cre_ref[pl.ds(g, 1), :]
        ci = cim_ref[pl.ds(g, 1), :]
        br, bi = btre_ref[g], btim_ref[g]
        bbre_ref[g] = cr * br - ci * bi
        bbim_ref[g] = cr * bi + ci * br
        return carry

    lax.fori_loop(0, SSM_GROUPS, per_group, 0)


def _ssm_params(lam_re, lam_im, log_dt, b_re, b_im, c_re, c_im, d_skip):
    g, n, p = SSM_GROUPS, SSM_STATE, SSM_GROUP
    bt_re = jnp.swapaxes(b_re, 1, 2)
    bt_im = jnp.swapaxes(b_im, 1, 2)
    are, aim, bbre, bbim = pl.pallas_call(
        _ssm_prep_body,
        out_shape=(jax.ShapeDtypeStruct((g, n), F32), jax.ShapeDtypeStruct((g, n), F32),
                   jax.ShapeDtypeStruct((g, p, n), F32), jax.ShapeDtypeStruct((g, p, n), F32)),
        scratch_shapes=[pltpu.VMEM((g, n), F32), pltpu.VMEM((g, n), F32)],
        name="ssm_prep",
    )(lam_re, lam_im, log_dt.reshape(g, 1), bt_re, bt_im)

    t, tg = SSM_TILES, SSM_TILE_GROUPS
    eye = jnp.eye(tg, dtype=F32)

    def bmat(bb):
        b5 = bb.reshape(t, tg, p, 1, n) * eye[None, :, None, :, None]
        return b5.reshape(t, tg * p, tg * n)

    def cmat(cc):
        c4 = jnp.transpose(cc.reshape(t, tg, p, n), (0, 1, 3, 2))
        c5 = c4[:, :, :, None, :] * eye[None, :, None, :, None]
        return c5.reshape(t, tg * n, tg * p)

    b_mat = jnp.concatenate([bmat(bbre), bmat(bbim)], axis=2).astype(BF16)
    c_mat = jnp.concatenate([cmat(c_re), -cmat(c_im)], axis=1).astype(BF16)
    return (b_mat, c_mat, are.reshape(t, 1, tg * n), aim.reshape(t, 1, tg * n),
            d_skip.reshape(t, 1, tg * p))


def _gelu_tanh(y):
    return 0.5 * y * (1.0 + jnp.tanh(math.sqrt(2.0 / math.pi) * (y + 0.044715 * (y * y * y))))


def _ssm_prompt_body(u_ref, b_ref, c_ref, are_ref, aim_ref, d_ref,
                     y_ref, hre_ref, him_ref,
                     bu_ref, h_ref, carry_ref, p_re_ref, p_im_ref, a1_re_ref, a1_im_ref):
    st = SSM_TILE_ST
    sub = V7X_SUBLANES
    nl = st // V7X_LANES
    tt = pl.program_id(1)

    @pl.when(tt == 0)
    def _init():
        carry_ref[...] = jnp.zeros_like(carry_ref)
        ar = jnp.broadcast_to(are_ref[0], (sub, st))
        ai = jnp.broadcast_to(aim_ref[0], (sub, st))
        lo = lax.broadcasted_iota(jnp.int32, (sub, st), 0) < BATCH
        p_re_ref[...] = jnp.where(lo, ar, ar * ar - ai * ai)
        p_im_ref[...] = jnp.where(lo, ai, 2.0 * ar * ai)
        a1_re_ref[...] = jnp.where(lo, 0.0, ar)
        a1_im_ref[...] = jnp.where(lo, 0.0, ai)

    u = u_ref[...]
    bu_ref[...] = jnp.dot(u.astype(BF16), b_ref[0], preferred_element_type=F32)

    lo8 = lax.broadcasted_iota(jnp.int32, (sub, V7X_LANES), 0) < BATCH
    rows = u_ref.shape[0]

    def step(i, carry):
        r0 = pl.multiple_of(i * sub, sub)
        new = []
        for j in range(nl):
            lre = slice(j * V7X_LANES, (j + 1) * V7X_LANES)
            lim = slice(st + j * V7X_LANES, st + (j + 1) * V7X_LANES)
            xr = bu_ref[pl.ds(r0, sub), lre]
            xi = bu_ref[pl.ds(r0, sub), lim]
            a1r, a1i = a1_re_ref[:, lre], a1_im_ref[:, lre]
            pr, pi = p_re_ref[:, lre], p_im_ref[:, lre]
            sxr = pltpu.roll(xr, BATCH, 0)
            sxi = pltpu.roll(xi, BATCH, 0)
            yr = xr + a1r * sxr - a1i * sxi
            yi = xi + a1r * sxi + a1i * sxr
            hr, hi = carry[2 * j], carry[2 * j + 1]
            hbr = jnp.where(lo8, pltpu.roll(hr, BATCH, 0), hr)
            hbi = jnp.where(lo8, pltpu.roll(hi, BATCH, 0), hi)
            nr = yr + pr * hbr - pi * hbi
            ni = yi + pr * hbi + pi * hbr
            h_ref[pl.ds(r0, sub), lre] = nr
            h_ref[pl.ds(r0, sub), lim] = ni
            new += [nr, ni]
        return tuple(new)

    init = []
    for j in range(nl):
        init += [carry_ref[:, j * V7X_LANES:(j + 1) * V7X_LANES],
                 carry_ref[:, st + j * V7X_LANES:st + (j + 1) * V7X_LANES]]
    final = lax.fori_loop(0, rows // sub, step, tuple(init))
    for j in range(nl):
        carry_ref[:, j * V7X_LANES:(j + 1) * V7X_LANES] = final[2 * j]
        carry_ref[:, st + j * V7X_LANES:st + (j + 1) * V7X_LANES] = final[2 * j + 1]

    y = jnp.dot(h_ref[...].astype(BF16), c_ref[0], preferred_element_type=F32) + d_ref[0] * u
    y_ref[...] = _gelu_tanh(y).astype(y_ref.dtype)
    hre_ref[...] = carry_ref[:, 0:st]
    him_ref[...] = carry_ref[:, st:2 * st]


def _ssm_prompt(u, b_mat, c_mat, are, aim, d):
    tsteps = 128
    rows = tsteps * BATCH
    nt = SEQ // tsteps
    st, ch = SSM_TILE_ST, SSM_TILE_CH
    est = (2 * rows * ch * 4 + 2 * rows * ch * 2 + 4 * ch * 2 * st * 2 + 2 * rows * 2 * st * 4
           + rows * 2 * st * 2 + 8 * 8 * st * 4 + (4 << 20))
    return pl.pallas_call(
        _ssm_prompt_body,
        out_shape=(jax.ShapeDtypeStruct((SEQ * BATCH, D_MODEL), BF16),
                   jax.ShapeDtypeStruct((V7X_SUBLANES, SSM_GROUPS * SSM_STATE), F32),
                   jax.ShapeDtypeStruct((V7X_SUBLANES, SSM_GROUPS * SSM_STATE), F32)),
        grid=(SSM_TILES, nt),
        in_specs=[pl.BlockSpec((rows, ch), lambda g, t: (t, g)),
                  pl.BlockSpec((1, ch, 2 * st), lambda g, t: (g, 0, 0)),
                  pl.BlockSpec((1, 2 * st, ch), lambda g, t: (g, 0, 0)),
                  pl.BlockSpec((1, 1, st), lambda g, t: (g, 0, 0)),
                  pl.BlockSpec((1, 1, st), lambda g, t: (g, 0, 0)),
                  pl.BlockSpec((1, 1, ch), lambda g, t: (g, 0, 0))],
        out_specs=(pl.BlockSpec((rows, ch), lambda g, t: (t, g)),
                   pl.BlockSpec((V7X_SUBLANES, st), lambda g, t: (0, g)),
                   pl.BlockSpec((V7X_SUBLANES, st), lambda g, t: (0, g))),
        scratch_shapes=[pltpu.VMEM((rows, 2 * st), F32), pltpu.VMEM((rows, 2 * st), F32),
                        pltpu.VMEM((V7X_SUBLANES, 2 * st), F32),
                        pltpu.VMEM((V7X_SUBLANES, st), F32), pltpu.VMEM((V7X_SUBLANES, st), F32),
                        pltpu.VMEM((V7X_SUBLANES, st), F32), pltpu.VMEM((V7X_SUBLANES, st), F32)],
        compiler_params=_cparams(2, est),
        name="ssm_prompt",
    )(u, b_mat, c_mat, are, aim, d)


def _ssm_sample_body(u_ref, h0re_ref, h0im_ref, b_ref, c_ref, are_ref, aim_ref, d_ref,
                     y_ref, hre_ref, him_ref):
    st = SSM_TILE_ST
    u = u_ref[...]
    bu = jnp.dot(u.astype(BF16), b_ref[0], preferred_element_type=F32)
    ar, ai = are_ref[0], aim_ref[0]
    h0r, h0i = h0re_ref[...], h0im_ref[...]
    hr = ar * h0r - ai * h0i + bu[:, 0:st]
    hi = ar * h0i + ai * h0r + bu[:, st:2 * st]
    hre_ref[...] = hr
    him_ref[...] = hi
    hcat = jnp.concatenate([hr, hi], axis=1).astype(BF16)
    y = jnp.dot(hcat, c_ref[0], preferred_element_type=F32) + d_ref[0] * u
    y_ref[...] = _gelu_tanh(y).astype(y_ref.dtype)


def _ssm_sample(u, h0_re, h0_im, b_mat, c_mat, are, aim, d):
    nb = DEC_BATCH
    st, ch = SSM_TILE_ST, SSM_TILE_CH
    est = 2 * (nb * ch * 6 + 4 * nb * st * 4 + 2 * ch * 2 * st * 2) + (4 << 20)
    return pl.pallas_call(
        _ssm_sample_body,
        out_shape=(jax.ShapeDtypeStruct((nb, D_MODEL), BF16),
                   jax.ShapeDtypeStruct((nb, SSM_GROUPS * SSM_STATE), F32),
                   jax.ShapeDtypeStruct((nb, SSM_GROUPS * SSM_STATE), F32)),
        grid=(SSM_TILES,),
        in_specs=[pl.BlockSpec((nb, ch), lambda g: (0, g)),
                  pl.BlockSpec((nb, st), lambda g: (0, g)),
                  pl.BlockSpec((nb, st), lambda g: (0, g)),
                  pl.BlockSpec((1, ch, 2 * st), lambda g: (g, 0, 0)),
                  pl.BlockSpec((1, 2 * st, ch), lambda g: (g, 0, 0)),
                  pl.BlockSpec((1, 1, st), lambda g: (g, 0, 0)),
                  pl.BlockSpec((1, 1, st), lambda g: (g, 0, 0)),
                  pl.BlockSpec((1, 1, ch), lambda g: (g, 0, 0))],
        out_specs=(pl.BlockSpec((nb, ch), lambda g: (0, g)),
                   pl.BlockSpec((nb, st), lambda g: (0, g)),
                   pl.BlockSpec((nb, st), lambda g: (0, g))),
        compiler_params=_cparams(1, est),
        name="ssm_sample",
    )(u, h0_re, h0_im, b_mat, c_mat, are, aim, d)


def kernel(x_prompt, x_sample, cache_swa_k, cache_swa_v, state_conv, state_ssm_re, state_ssm_im, ffn_norm, ffn_w_gu, ffn_w_down, mix_norm, even_w_in, even_q_norm, even_k_norm, even_sinks, even_glu_b, even_conv_w, even_conv_b, even_ln_g, even_ln_b, even_w_out, odd_w_in, odd_lam_re, odd_lam_im, odd_log_dt, odd_b_re, odd_b_im, odd_c_re, odd_c_im, odd_d, odd_w_gate, odd_w_out):
    mp, ms = BATCH * SEQ, DEC_BATCH
    xp = x_prompt.reshape(mp, D_MODEL)
    xs = x_sample.reshape(ms, D_MODEL)
    tm_p, tmd_p = 1024, 512

    ssm = _ssm_params(odd_lam_re[0], odd_lam_im[0], odd_log_dt[0], odd_b_re[0], odd_b_im[0],
                      odd_c_re[0], odd_c_im[0], odd_d[0])

    xp = _half_ffn(xp, ffn_norm[0, 0], ffn_w_gu, ffn_w_down, (0, 0), tm=tm_p, tm_down=tmd_p)
    xs = _half_ffn(xs, ffn_norm[0, 0], ffn_w_gu, ffn_w_down, (0, 0), tm=ms, tm_down=ms)

    even_p = (even_sinks[0], even_q_norm[0], even_k_norm[0], even_glu_b[0], even_conv_w[0],
              even_conv_b[0], even_ln_g[0], even_ln_b[0])
    hp = _rmsnorm(xp, mix_norm[0])
    zp = _matmul(hp, even_w_in, (0,), n_cols=EVEN_IN, tm=tm_p, tn=512, name="even_in")
    mixp, pk, pv, pc = _even_core_prompt(zp, *even_p)
    xp = _matmul(mixp, even_w_out, (0,), n_cols=D_MODEL, tm=tm_p, tn=512, epilogue="res", extra=xp,
                 name="even_out")

    hs = _rmsnorm(xs, mix_norm[0])
    zs = _matmul(hs, even_w_in, (0,), n_cols=EVEN_IN, tm=ms, tn=512, name="even_in")
    mixs, sk, sv, sc = _even_core_sample(
        zs, cache_swa_k.reshape(ms, WINDOW, KV_COLS), cache_swa_v.reshape(ms, WINDOW, KV_COLS),
        state_conv.reshape(ms, CONV_WIDTH - 1, CONV_CH), *even_p)
    xs = _matmul(mixs.reshape(ms, D_MODEL), even_w_out, (0,), n_cols=D_MODEL, tm=ms, tn=512,
                 epilogue="res", extra=xs, name="even_out")

    xp = _half_ffn(xp, ffn_norm[0, 1], ffn_w_gu, ffn_w_down, (0, 1), tm=tm_p, tm_down=tmd_p,
                   out_tmajor=True)
    xs = _half_ffn(xs, ffn_norm[0, 1], ffn_w_gu, ffn_w_down, (0, 1), tm=ms, tm_down=ms)

    xp = _half_ffn(xp, ffn_norm[1, 0], ffn_w_gu, ffn_w_down, (1, 0), tm=tm_p, tm_down=tmd_p)
    xs = _half_ffn(xs, ffn_norm[1, 0], ffn_w_gu, ffn_w_down, (1, 0), tm=ms, tm_down=ms)

    hp = _rmsnorm(xp, mix_norm[1])
    up = _matmul(hp, odd_w_in, (0,), n_cols=D_MODEL, tm=tm_p, tn=512, name="odd_in")
    yp, pre8, pim8 = _ssm_prompt(up, *ssm)
    ygp = _matmul(yp, odd_w_gate, (0,), n_cols=D_MODEL, tm=tm_p, tn=512, epilogue="gate",
                  out_dtype=BF16, extra=yp, name="odd_gate")
    xp = _matmul(ygp, odd_w_out, (0,), n_cols=D_MODEL, tm=tm_p, tn=512, epilogue="res", extra=xp,
                 name="odd_out")

    hs = _rmsnorm(xs, mix_norm[1])
    us = _matmul(hs, odd_w_in, (0,), n_cols=D_MODEL, tm=ms, tn=512, name="odd_in")
    ys, sre, sim = _ssm_sample(us, state_ssm_re.reshape(ms, -1), state_ssm_im.reshape(ms, -1), *ssm)
    ygs = _matmul(ys, odd_w_gate, (0,), n_cols=D_MODEL, tm=ms, tn=512, epilogue="gate",
                  out_dtype=BF16, extra=ys, name="odd_gate")
    xs = _matmul(ygs, odd_w_out, (0,), n_cols=D_MODEL, tm=ms, tn=512, epilogue="res", extra=xs,
                 name="odd_out")

    xp = _half_ffn(xp, ffn_norm[1, 1], ffn_w_gu, ffn_w_down, (1, 1), tm=tm_p, tm_down=tmd_p,
                   in_tmajor=True)
    xs = _half_ffn(xs, ffn_norm[1, 1], ffn_w_gu, ffn_w_down, (1, 1), tm=ms, tm_down=ms)

    kv5 = lambda t, b: t.reshape(1, b, WINDOW, N_KV_HEADS, HEAD_DIM)
    st4 = lambda t, b: t.reshape(1, b, SSM_GROUPS, SSM_STATE)
    return (xp.reshape(BATCH, SEQ, D_MODEL), xs.reshape(DEC_BATCH, 1, D_MODEL),
            kv5(pk, BATCH), kv5(pv, BATCH), pc.reshape(1, BATCH, CONV_WIDTH - 1, CONV_CH),
            st4(pre8[BATCH:2 * BATCH], BATCH), st4(pim8[BATCH:2 * BATCH], BATCH),
            kv5(sk, DEC_BATCH), kv5(sv, DEC_BATCH), sc.reshape(1, DEC_BATCH, CONV_WIDTH - 1, CONV_CH),
            st4(sre, DEC_BATCH), st4(sim, DEC_BATCH))
```

```python
import functools
import math

import jax
import jax.numpy as jnp
from jax import lax
from jax.experimental import pallas as pl
from jax.experimental.pallas import tpu as pltpu

F32 = jnp.float32
BF16 = jnp.bfloat16

D_MODEL = 4096
BATCH = 4
SEQ = 2048
DEC_BATCH = 32
PAST_LEN = 16384
HEAD_DIM = 128
N_HEADS = 16
N_KV_HEADS = 4
Q_PER_KV = 4
WINDOW = 128
ROT_DIM = 32
ROPE_THETA = 500000.0
ATT_WIDTH = 2048
CONV_CH = 2048
CONV_WIDTH = 31
Q_COLS = 2048
KV_COLS = 512
EVEN_IN = 7168
SSM_GROUPS = 256
SSM_GROUP = 16
SSM_STATE = 64
D_FF = 11008
EPS = 1e-6
NEG_INF = -1e30

V7X_VMEM_BYTES = 64 * 1024 * 1024
V7X_LANES = 128
V7X_SUBLANES = 8
VMEM_CAP = V7X_VMEM_BYTES - 6 * 1024 * 1024

PROMPT_ROWS = BATCH * SEQ
MERGE_TILE = 128
M_ROWS = PROMPT_ROWS + MERGE_TILE
ROW_TILES = 8
TM = M_ROWS // ROW_TILES
TM_DOWN = TM // 2

SSM_TILE_GROUPS = 16
SSM_TILE_CH = SSM_TILE_GROUPS * SSM_GROUP
SSM_TILE_ST = SSM_TILE_GROUPS * SSM_STATE
SSM_TILES = SSM_GROUPS // SSM_TILE_GROUPS
SSM_STEPS = 256
CONV_HALO = 32


def _cparams(n_grid, est_bytes):
    limit = int(min(VMEM_CAP, est_bytes * 5 // 4 + (4 << 20)))
    return pltpu.CompilerParams(dimension_semantics=("arbitrary",) * n_grid,
                                vmem_limit_bytes=limit)


def _sigmoid(x):
    return 1.0 / (1.0 + jnp.exp(-x))


def _rms(x, g):
    ms = jnp.mean(x * x, axis=-1, keepdims=True)
    return x * lax.rsqrt(ms + EPS) * g


def _rmsnorm_body(x_ref, g_ref, o_ref):
    o_ref[...] = _rms(x_ref[...], g_ref[...]).astype(o_ref.dtype)


def _rmsnorm(x, g):
    d = x.shape[1]
    tr = TM // 5
    return pl.pallas_call(
        _rmsnorm_body,
        out_shape=jax.ShapeDtypeStruct((M_ROWS, d), BF16),
        grid=(M_ROWS // tr,),
        in_specs=[pl.BlockSpec((tr, d), lambda i: (i, 0)),
                  pl.BlockSpec((1, d), lambda i: (0, 0))],
        out_specs=pl.BlockSpec((tr, d), lambda i: (i, 0)),
        compiler_params=_cparams(1, 2 * tr * d * 4 + 2 * tr * d * 2),
        name="rmsnorm",
    )(x, g.reshape(1, d))


def _merge_rmsnorm_body(xp_ref, xs_ref, g_ref, x_ref, h_ref):
    i = pl.program_id(0)
    g = g_ref[...]

    @pl.when(i < PROMPT_ROWS // MERGE_TILE)
    def _prompt():
        x = xp_ref[...]
        x_ref[...] = x
        h_ref[...] = _rms(x, g).astype(h_ref.dtype)

    @pl.when(i == PROMPT_ROWS // MERGE_TILE)
    def _sample():
        xs = xs_ref[...]
        pad = MERGE_TILE - DEC_BATCH
        x_ref[0:DEC_BATCH, :] = xs
        x_ref[DEC_BATCH:MERGE_TILE, :] = jnp.zeros((pad, D_MODEL), F32)
        h_ref[0:DEC_BATCH, :] = _rms(xs, g).astype(h_ref.dtype)
        h_ref[DEC_BATCH:MERGE_TILE, :] = jnp.zeros((pad, D_MODEL), h_ref.dtype)


def _merge_rmsnorm(xp, xs, g):
    d = D_MODEL
    last = PROMPT_ROWS // MERGE_TILE - 1
    return pl.pallas_call(
        _merge_rmsnorm_body,
        out_shape=(jax.ShapeDtypeStruct((M_ROWS, d), F32), jax.ShapeDtypeStruct((M_ROWS, d), BF16)),
        grid=(M_ROWS // MERGE_TILE,),
        in_specs=[pl.BlockSpec((MERGE_TILE, d), lambda i: (jnp.minimum(i, last), 0)),
                  pl.BlockSpec((DEC_BATCH, d), lambda i: (0, 0)),
                  pl.BlockSpec((1, d), lambda i: (0, 0))],
        out_specs=(pl.BlockSpec((MERGE_TILE, d), lambda i: (i, 0)),
                   pl.BlockSpec((MERGE_TILE, d), lambda i: (i, 0))),
        compiler_params=_cparams(1, 4 * MERGE_TILE * d * 4 + 2 * MERGE_TILE * d * 2 + 2 * DEC_BATCH * d * 4),
        name="merge_rmsnorm",
    )(xp, xs, g.reshape(1, d))


def _mm_body(*refs, n_w, cast_w, epilogue, scale, side_cast, split_rows):
    a_ref = refs[0]
    w_refs = refs[1:1 + n_w]
    pos = 1 + n_w
    extra_ref = side_in_ref = side_out_ref = o2_ref = None
    if epilogue in ("res", "gate"):
        extra_ref = refs[pos]
        pos += 1
    if side_cast:
        side_in_ref = refs[pos]
        pos += 1
    o_ref = refs[pos]
    pos += 1
    if side_cast:
        side_out_ref = refs[pos]
        pos += 1
    if split_rows is not None:
        o2_ref = refs[pos]
        pos += 1
    wb_refs = refs[pos:pos + n_w] if cast_w else w_refs

    if cast_w:
        @pl.when(pl.program_id(1) == 0)
        def _cast_weights():
            for w_ref, wb_ref in zip(w_refs, wb_refs):
                wb_ref[...] = w_ref[...].astype(BF16)

    if side_cast:
        side_out_ref[...] = side_in_ref[...].astype(side_out_ref.dtype)

    a = a_ref[...].astype(BF16)
    accs = [jnp.dot(a, wb_ref[...], preferred_element_type=F32) for wb_ref in wb_refs]
    if epilogue == "plain":
        out = accs[0]
    elif epilogue == "swiglu":
        gate, up = accs
        out = gate * _sigmoid(gate) * up
    elif epilogue == "res":
        out = extra_ref[...] + scale * accs[0]
    elif epilogue == "gate":
        out = extra_ref[...].astype(F32) * _sigmoid(accs[0])
    o_ref[...] = out.astype(o_ref.dtype)

    if split_rows is not None:
        @pl.when(pl.program_id(1) == pl.num_programs(1) - 1)
        def _sample_rows():
            o2_ref[...] = out[split_rows:split_rows + DEC_BATCH, :].astype(o2_ref.dtype)


def _matmul(a, w, widx=(), *, n_cols, tm=TM, tn=512, epilogue="plain", out_dtype=F32, col_offs=(0,),
            extra=None, scale=1.0, side=None, split_out=False, name="mm"):
    k = a.shape[1]
    n_w = len(col_offs)
    cast_w = w.dtype != BF16
    nlead = len(widx)
    n_i = M_ROWS // tm
    n_j = n_cols // tn

    operands = [a]
    in_specs = [pl.BlockSpec((tm, k), lambda j, i: (i, 0))]
    for off in col_offs:
        operands.append(w)
        in_specs.append(pl.BlockSpec((None,) * nlead + (k, tn),
                                     lambda j, i, off=off: tuple(widx) + (0, j + off)))
    w_bytes = w.dtype.itemsize
    est = 2 * tm * k * a.dtype.itemsize + n_w * (2 * k * tn * w_bytes + (k * tn * 2 if cast_w else 0))
    if extra is not None:
        operands.append(extra)
        in_specs.append(pl.BlockSpec((tm, tn), lambda j, i: (i, j)))
        est += 2 * tm * tn * extra.dtype.itemsize
    out_shapes, out_specs = [], []
    if split_out:
        out_shapes.append(jax.ShapeDtypeStruct((PROMPT_ROWS, n_cols), out_dtype))
    else:
        out_shapes.append(jax.ShapeDtypeStruct((M_ROWS, n_cols), out_dtype))
    out_specs.append(pl.BlockSpec((tm, tn), lambda j, i: (i, j)))
    if side is not None:
        w_src, widx_src, rows_total = side
        slab = rows_total // (n_i * n_j)
        assert slab * n_i * n_j == rows_total and slab % 16 == 0
        cols_src = w_src.shape[-1]
        operands.append(w_src)
        in_specs.append(pl.BlockSpec((None,) * len(widx_src) + (slab, cols_src),
                                     lambda j, i: tuple(widx_src) + (j * n_i + i, 0)))
        out_shapes.append(jax.ShapeDtypeStruct((rows_total, cols_src), BF16))
        out_specs.append(pl.BlockSpec((slab, cols_src), lambda j, i: (j * n_i + i, 0)))
        est += 2 * slab * cols_src * 6
    split_rows = None
    if split_out:
        split_rows = PROMPT_ROWS - (n_i - 1) * tm
        assert split_rows % V7X_SUBLANES == 0 and 0 <= split_rows <= tm - DEC_BATCH
        out_shapes.append(jax.ShapeDtypeStruct((DEC_BATCH, n_cols), out_dtype))
        out_specs.append(pl.BlockSpec((DEC_BATCH, tn), lambda j, i: (0, j)))
    est += 2 * tm * tn * jnp.dtype(out_dtype).itemsize + (n_w + 1) * tm * tn * 4

    outs = pl.pallas_call(
        functools.partial(_mm_body, n_w=n_w, cast_w=cast_w, epilogue=epilogue, scale=scale,
                          side_cast=side is not None, split_rows=split_rows),
        out_shape=tuple(out_shapes),
        grid=(n_j, n_i),
        in_specs=in_specs,
        out_specs=tuple(out_specs),
        scratch_shapes=[pltpu.VMEM((k, tn), BF16) for _ in range(n_w)] if cast_w else [],
        compiler_params=_cparams(2, est),
        name=name,
    )(*operands)
    return outs[0] if len(outs) == 1 else outs


def _half_ffn(x, h, w_gu, w_down, widx, *, split_out=False):
    act, wd_bf16 = _matmul(h, w_gu, widx, n_cols=D_FF, tn=256, epilogue="swiglu", out_dtype=BF16,
                           col_offs=(0, D_FF // 256), side=(w_down, widx, D_FF), name="ffn_gu")
    return _matmul(act, wd_bf16, n_cols=D_MODEL, tm=TM_DOWN, tn=512, epilogue="res", extra=x, scale=0.5,
                   split_out=split_out, name="ffn_down")


def _rope_tables(pos):
    half = ROT_DIM // 2
    inv_freq = jnp.power(jnp.float32(ROPE_THETA), -jnp.arange(half, dtype=F32) * (2.0 / ROT_DIM))
    ang = pos.astype(F32)[:, None] * inv_freq[None, :]
    cos, sin = jnp.cos(ang), jnp.sin(ang)
    n = pos.shape[0]
    rest = HEAD_DIM - ROT_DIM
    c = jnp.concatenate([cos, cos, jnp.ones((n, rest), F32)], axis=1)
    sp = jnp.concatenate([jnp.zeros((n, half), F32), sin, jnp.zeros((n, rest), F32)], axis=1)
    sm = jnp.concatenate([-sin, jnp.zeros((n, HEAD_DIM - half), F32)], axis=1)
    return c, sp, sm


def _norm_rope(x, g, c, sp, sm):
    half = ROT_DIM // 2
    y = x * lax.rsqrt(jnp.mean(x * x, axis=-1, keepdims=True) + EPS) * g
    return y * c + pltpu.roll(y, half, 1) * sp + pltpu.roll(y, HEAD_DIM - half, 1) * sm


def _layer_norm_silu(c, g, b):
    mu = jnp.mean(c, axis=-1, keepdims=True)
    xc = c - mu
    var = jnp.mean(xc * xc, axis=-1, keepdims=True)
    y = xc * lax.rsqrt(var + EPS) * g + b
    return y * _sigmoid(y)


def _even_prompt_body(sinks_ref, q_ref, kv_ref, a0_ref, a1_ref, g0_ref, g1_ref,
                      c_ref, sp_ref, sm_ref, qn_ref, kn_ref, glub_ref, cw_ref, cb_ref,
                      lng_ref, lnb_ref,
                      mix_ref, nk_ref, nv_ref, nc_ref,
                      kprev_ref, vprev_ref, gbuf_ref, cbuf_ref):
    blk = WINDOW
    i = pl.program_id(1)

    @pl.when(i == 0)
    def _reset():
        kprev_ref[...] = jnp.zeros_like(kprev_ref)
        vprev_ref[...] = jnp.zeros_like(vprev_ref)
        gbuf_ref[0:CONV_HALO, :] = jnp.zeros((CONV_HALO, CONV_CH), F32)

    c, sp, sm = c_ref[...], sp_ref[...], sm_ref[...]
    qn, kn = qn_ref[...], kn_ref[...]

    row = lax.broadcasted_iota(jnp.int32, (blk, 2 * blk), 0)
    col = lax.broadcasted_iota(jnp.int32, (blk, 2 * blk), 1)
    first_key = jnp.where(i > 0, 0, blk)
    mask = (col >= jnp.maximum(row, first_key)) & (col <= row + WINDOW)
    scale = HEAD_DIM ** -0.5
    for g in range(N_KV_HEADS):
        ks = slice(g * HEAD_DIM, (g + 1) * HEAD_DIM)
        k_cur = _norm_rope(kv_ref[:, ks], kn, c, sp, sm)
        v_cur = kv_ref[:, KV_COLS + g * HEAD_DIM:KV_COLS + (g + 1) * HEAD_DIM]
        nk_ref[0, :, ks] = k_cur
        nv_ref[0, :, ks] = v_cur
        k_cur_b = k_cur.astype(BF16)
        v_cur_b = v_cur.astype(BF16)
        k2 = jnp.concatenate([kprev_ref[:, ks], k_cur_b], axis=0)
        v2 = jnp.concatenate([vprev_ref[:, ks], v_cur_b], axis=0)
        for r in range(Q_PER_KV):
            h = g * Q_PER_KV + r
            hs = slice(h * HEAD_DIM, (h + 1) * HEAD_DIM)
            qh = _norm_rope(q_ref[:, hs], qn, c, sp, sm).astype(BF16)
            s = lax.dot_general(qh, k2, (((1,), (1,)), ((), ())),
                                preferred_element_type=F32) * scale
            s = jnp.where(mask, s, NEG_INF)
            sink = sinks_ref[h]
            m = jnp.maximum(jnp.max(s, axis=-1, keepdims=True), sink)
            p = jnp.exp(s - m)
            denom = jnp.sum(p, axis=-1, keepdims=True) + jnp.exp(sink - m)
            w = (p / denom).astype(BF16)
            o = jnp.dot(w, v2, preferred_element_type=F32)
            mix_ref[:, hs] = o.astype(mix_ref.dtype)
        kprev_ref[:, ks] = k_cur_b
        vprev_ref[:, ks] = v_cur_b

    half_ch = CONV_CH // 2
    for hh, (a_ref, g_ref) in enumerate(((a0_ref, g0_ref), (a1_ref, g1_ref))):
        cs = slice(hh * half_ch, (hh + 1) * half_ch)
        a = a_ref[...] + glub_ref[:, cs]
        gate = g_ref[...] + glub_ref[:, CONV_CH + hh * half_ch:CONV_CH + (hh + 1) * half_ch]
        gbuf_ref[CONV_HALO:CONV_HALO + blk, cs] = a * _sigmoid(gate)

    lane_chunk = 2 * V7X_LANES
    first = CONV_HALO - (CONV_WIDTH - 1)

    def conv_chunk(ci, carry):
        ls = pl.ds(pl.multiple_of(ci * lane_chunk, lane_chunk), lane_chunk)
        acc = jnp.zeros((blk, lane_chunk), F32)
        for w in range(CONV_WIDTH):
            acc = acc + cw_ref[w:w + 1, ls] * gbuf_ref[first + w:first + w + blk, ls]
        cbuf_ref[:, ls] = acc
        return carry

    lax.fori_loop(0, CONV_CH // lane_chunk, conv_chunk, 0)
    y = _layer_norm_silu(cbuf_ref[...] + cb_ref[...], lng_ref[...], lnb_ref[...])
    mix_ref[:, ATT_WIDTH:ATT_WIDTH + CONV_CH] = y.astype(mix_ref.dtype)

    @pl.when(i == pl.num_programs(1) - 1)
    def _emit_conv_state():
        nc_ref[0] = gbuf_ref[CONV_HALO + blk - (CONV_WIDTH - 1):CONV_HALO + blk, :]

    gbuf_ref[0:CONV_HALO, :] = gbuf_ref[blk:blk + CONV_HALO, :]


def _even_core_prompt(z, sinks, qn, kn, glub, cw, cb, lng, lnb):
    nb = SEQ // WINDOW
    blk = WINDOW
    c, sp, sm = _rope_tables(jnp.arange(SEQ, dtype=jnp.int32))
    rowmap = lambda col: (lambda b, i: (b * nb + i, col))
    tab = pl.BlockSpec((blk, HEAD_DIM), lambda b, i: (i, 0))
    vec = lambda n: pl.BlockSpec((1, n), lambda b, i: (0, 0))
    est = (2 * blk * (2048 + 1024 + 4 * 1024) * 4 + 2 * blk * 4096 * 2 + 4 * blk * 512 * 4
           + (blk + CONV_HALO) * CONV_CH * 4 + 3 * blk * CONV_CH * 4 + 2 * CONV_WIDTH * CONV_CH * 4 + (8 << 20))
    return pl.pallas_call(
        _even_prompt_body,
        out_shape=(jax.ShapeDtypeStruct((M_ROWS, 2 * ATT_WIDTH), BF16),
                   jax.ShapeDtypeStruct((BATCH, blk, KV_COLS), F32),
                   jax.ShapeDtypeStruct((BATCH, blk, KV_COLS), F32),
                   jax.ShapeDtypeStruct((BATCH, CONV_WIDTH - 1, CONV_CH), F32)),
        grid=(BATCH, nb),
        in_specs=[pl.BlockSpec(memory_space=pltpu.SMEM),
                  pl.BlockSpec((blk, 2048), rowmap(0)),
                  pl.BlockSpec((blk, 1024), rowmap(2)),
                  pl.BlockSpec((blk, 1024), rowmap(3)),
                  pl.BlockSpec((blk, 1024), rowmap(4)),
                  pl.BlockSpec((blk, 1024), rowmap(5)),
                  pl.BlockSpec((blk, 1024), rowmap(6)),
                  tab, tab, tab, vec(HEAD_DIM), vec(HEAD_DIM), vec(2 * CONV_CH),
                  pl.BlockSpec((CONV_WIDTH, CONV_CH), lambda b, i: (0, 0)),
                  vec(CONV_CH), vec(CONV_CH), vec(CONV_CH)],
        out_specs=(pl.BlockSpec((blk, 2 * ATT_WIDTH), rowmap(0)),
                   pl.BlockSpec((1, blk, KV_COLS), lambda b, i: (b, 0, 0)),
                   pl.BlockSpec((1, blk, KV_COLS), lambda b, i: (b, 0, 0)),
                   pl.BlockSpec((1, CONV_WIDTH - 1, CONV_CH), lambda b, i: (b, 0, 0))),
        scratch_shapes=[pltpu.VMEM((blk, KV_COLS), BF16), pltpu.VMEM((blk, KV_COLS), BF16),
                        pltpu.VMEM((blk + CONV_HALO, CONV_CH), F32), pltpu.VMEM((blk, CONV_CH), F32)],
        compiler_params=_cparams(2, est),
        name="even_core_prompt",
    )(sinks, z, z, z, z, z, z, c, sp, sm, qn.reshape(1, -1), kn.reshape(1, -1), glub.reshape(1, -1),
      cw, cb.reshape(1, -1), lng.reshape(1, -1), lnb.reshape(1, -1))


def _even_sample_body(sinks_ref, z_ref, ck_ref, cv_ref, sc_ref, c_ref, sp_ref, sm_ref, qn_ref, kn_ref,
                      glub_ref, cw_ref, cb_ref, lng_ref, lnb_ref,
                      mix_ref, nk_ref, nv_ref, nc_ref):
    zrow = lambda lo, hi: z_ref[0, :, lo:hi]
    c, sp, sm = c_ref[...], sp_ref[...], sm_ref[...]
    qn, kn = qn_ref[...], kn_ref[...]
    nbuf = WINDOW
    scale = HEAD_DIM ** -0.5
    qrow = lax.broadcasted_iota(jnp.int32, (V7X_SUBLANES, 1), 0)
    krow = lax.broadcasted_iota(jnp.int32, (nbuf, HEAD_DIM), 0)
    for g in range(N_KV_HEADS):
        ks = slice(g * HEAD_DIM, (g + 1) * HEAD_DIM)
        k_new = _norm_rope(zrow(Q_COLS + g * HEAD_DIM, Q_COLS + (g + 1) * HEAD_DIM), kn, c, sp, sm)
        v_new = zrow(Q_COLS + KV_COLS + g * HEAD_DIM, Q_COLS + KV_COLS + (g + 1) * HEAD_DIM)
        k_old = ck_ref[0, :, ks]
        v_old = cv_ref[0, :, ks]
        nk_ref[0, :, ks] = jnp.where(krow == nbuf - 1, k_new, pltpu.roll(k_old, nbuf - 1, 0))
        nv_ref[0, :, ks] = jnp.where(krow == nbuf - 1, v_new, pltpu.roll(v_old, nbuf - 1, 0))
        qs = [_norm_rope(zrow((g * Q_PER_KV + r) * HEAD_DIM, (g * Q_PER_KV + r + 1) * HEAD_DIM),
                         qn, c, sp, sm) for r in range(Q_PER_KV)]
        q8 = jnp.zeros((V7X_SUBLANES, HEAD_DIM), F32)
        for r in range(Q_PER_KV):
            q8 = jnp.where(qrow == r, qs[r], q8)
        q8b = q8.astype(BF16)
        s = lax.dot_general(q8b, k_old.astype(BF16), (((1,), (1,)), ((), ())),
                            preferred_element_type=F32) * scale
        k_new_r = k_new.astype(BF16).astype(F32)
        v_new_r = v_new.astype(BF16).astype(F32)
        s_self = jnp.sum(q8b.astype(F32) * k_new_r, axis=-1, keepdims=True) * scale
        sink = jnp.zeros((V7X_SUBLANES, 1), F32)
        for r in range(Q_PER_KV):
            sink = jnp.where(qrow == r, sinks_ref[g * Q_PER_KV + r], sink)
        m = jnp.maximum(jnp.maximum(jnp.max(s, axis=-1, keepdims=True), s_self), sink)
        p = jnp.exp(s - m)
        p_self = jnp.exp(s_self - m)
        denom = jnp.sum(p, axis=-1, keepdims=True) + p_self + jnp.exp(sink - m)
        o = jnp.dot((p / denom).astype(BF16), v_old.astype(BF16), preferred_element_type=F32)
        o = o + (p_self / denom).astype(BF16).astype(F32) * v_new_r
        for r in range(Q_PER_KV):
            h = g * Q_PER_KV + r
            mix_ref[0, :, h * HEAD_DIM:(h + 1) * HEAD_DIM] = o[r:r + 1, :]

    conv0 = Q_COLS + 2 * KV_COLS
    a = zrow(conv0, conv0 + CONV_CH) + glub_ref[:, 0:CONV_CH]
    gate = zrow(conv0 + CONV_CH, EVEN_IN) + glub_ref[:, CONV_CH:2 * CONV_CH]
    gl = a * _sigmoid(gate)
    nst = CONV_WIDTH - 1
    conv = jnp.sum(cw_ref[0:nst, :] * sc_ref[0], axis=0, keepdims=True) + cw_ref[nst:nst + 1, :] * gl
    y = _layer_norm_silu(conv + cb_ref[...], lng_ref[...], lnb_ref[...])
    mix_ref[0, :, ATT_WIDTH:ATT_WIDTH + CONV_CH] = y
    nc_ref[0, 0:nst - 1, :] = sc_ref[0, 1:nst, :]
    nc_ref[0, nst - 1:nst, :] = gl


def _even_core_sample(z, cache_k, cache_v, state_conv, sinks, qn, kn, glub, cw, cb, lng, lnb):
    nb = DEC_BATCH
    c, sp, sm = _rope_tables(jnp.full((1,), PAST_LEN, jnp.int32))
    vec = lambda n: pl.BlockSpec((1, n), lambda b: (0, 0))
    per_b = lambda r, n: pl.BlockSpec((1, r, n), lambda b: (b, 0, 0))
    nst = CONV_WIDTH - 1
    est = 2 * (EVEN_IN * 4 * nb + 4 * WINDOW * KV_COLS * 4 + 2 * 32 * CONV_CH * 4 + 4096 * 4 * 8) + (8 << 20)
    return pl.pallas_call(
        _even_sample_body,
        out_shape=(jax.ShapeDtypeStruct((nb, 1, 2 * ATT_WIDTH), F32),
                   jax.ShapeDtypeStruct((nb, WINDOW, KV_COLS), F32),
                   jax.ShapeDtypeStruct((nb, WINDOW, KV_COLS), F32),
                   jax.ShapeDtypeStruct((nb, nst, CONV_CH), F32)),
        grid=(nb,),
        in_specs=[pl.BlockSpec(memory_space=pltpu.SMEM),
                  per_b(1, EVEN_IN),
                  per_b(WINDOW, KV_COLS), per_b(WINDOW, KV_COLS), per_b(nst, CONV_CH),
                  vec(HEAD_DIM), vec(HEAD_DIM), vec(HEAD_DIM), vec(HEAD_DIM), vec(HEAD_DIM),
                  vec(2 * CONV_CH), pl.BlockSpec((CONV_WIDTH, CONV_CH), lambda b: (0, 0)),
                  vec(CONV_CH), vec(CONV_CH), vec(CONV_CH)],
        out_specs=(per_b(1, 2 * ATT_WIDTH), per_b(WINDOW, KV_COLS), per_b(WINDOW, KV_COLS),
                   per_b(nst, CONV_CH)),
        compiler_params=_cparams(1, est),
        name="even_core_sample",
    )(sinks, z[PROMPT_ROWS:PROMPT_ROWS + nb].reshape(nb, 1, EVEN_IN), cache_k, cache_v, state_conv, c, sp, sm,
      qn.reshape(1, -1), kn.reshape(1, -1), glub.reshape(1, -1), cw, cb.reshape(1, -1),
      lng.reshape(1, -1), lnb.reshape(1, -1))


def _put_sample_rows(merged, sample_rows):
    pad = jnp.zeros((MERGE_TILE - DEC_BATCH, merged.shape[1]), merged.dtype)
    tile = jnp.concatenate([sample_rows.astype(merged.dtype), pad], axis=0)
    return lax.dynamic_update_slice(merged, tile, (PROMPT_ROWS, 0))


def _ssm_prep_body(lre_ref, lim_ref, ldt_ref, btre_ref, btim_ref,
                   are_ref, aim_ref, bbre_ref, bbim_ref, cre_ref, cim_ref):
    lre, lim = lre_ref[...], lim_ref[...]
    dt = jnp.exp(ldt_ref[...])
    mag = jnp.exp(lre * dt)
    ang = lim * dt
    are = mag * jnp.cos(ang)
    aim = mag * jnp.sin(ang)
    are_ref[...] = are
    aim_ref[...] = aim
    nre, nim = are - 1.0, aim
    den = lre * lre + lim * lim
    cre_ref[...] = (nre * lre + nim * lim) / den
    cim_ref[...] = (nim * lre - nre * lim) / den

    def per_group(g, carry):
        cr = cre_ref[pl.ds(g, 1), :]
        ci = cim_ref[pl.ds(g, 1), :]
        br, bi = btre_ref[g], btim_ref[g]
        bbre_ref[g] = cr * br - ci * bi
        bbim_ref[g] = cr * bi + ci * br
        return carry

    lax.fori_loop(0, SSM_GROUPS, per_group, 0)


def _ssm_params(lam_re, lam_im, log_dt, b_re, b_im, c_re, c_im, d_skip):
    g, n, p = SSM_GROUPS, SSM_STATE, SSM_GROUP
    bt_re = jnp.swapaxes(b_re, 1, 2)
    bt_im = jnp.swapaxes(b_im, 1, 2)
    are, aim, bbre, bbim = pl.pallas_call(
        _ssm_prep_body,
        out_shape=(jax.ShapeDtypeStruct((g, n), F32), jax.ShapeDtypeStruct((g, n), F32),
                   jax.ShapeDtypeStruct((g, p, n), F32), jax.ShapeDtypeStruct((g, p, n), F32)),
        scratch_shapes=[pltpu.VMEM((g, n), F32), pltpu.VMEM((g, n), F32)],
        name="ssm_prep",
    )(lam_re, lam_im, log_dt.reshape(g, 1), bt_re, bt_im)

    t, tg = SSM_TILES, SSM_TILE_GROUPS
    eye = jnp.eye(tg, dtype=F32)

    def bmat(bb):
        b5 = bb.reshape(t, tg, p, 1, n) * eye[None, :, None, :, None]
        return b5.reshape(t, tg * p, tg * n)

    def cmat(cc):
        c4 = jnp.transpose(cc.reshape(t, tg, p, n), (0, 1, 3, 2))
        c5 = c4[:, :, :, None, :] * eye[None, :, None, :, None]
        return c5.reshape(t, tg * n, tg * p)

    b_mat = jnp.concatenate([bmat(bbre), bmat(bbim)], axis=2).astype(BF16)
    c_mat = jnp.concatenate([cmat(c_re), -cmat(c_im)], axis=1).astype(BF16)
    return (b_mat, c_mat, are.reshape(t, 1, tg * n), aim.reshape(t, 1, tg * n),
            d_skip.reshape(t, 1, tg * p))


def _gelu_tanh(y):
    return 0.5 * y * (1.0 + jnp.tanh(math.sqrt(2.0 / math.pi) * (y + 0.044715 * (y * y * y))))


def _ssm_prompt_body(u0_ref, u1_ref, u2_ref, u3_ref, b_ref, c_ref, are_ref, aim_ref, d_ref,
                     y_ref, hre_ref, him_ref,
                     us_ref, ys_ref, bu_ref, h_ref, carry_ref, p_re_ref, p_im_ref, a1_re_ref, a1_im_ref):
    st = SSM_TILE_ST
    sub = V7X_SUBLANES
    nl = st // V7X_LANES
    nh = SSM_TILE_CH // V7X_LANES
    tt = pl.program_id(1)
    steps = u0_ref.shape[0]
    rows = steps * BATCH

    @pl.when(tt == 0)
    def _init():
        carry_ref[...] = jnp.zeros_like(carry_ref)
        ar = jnp.broadcast_to(are_ref[0], (sub, st))
        ai = jnp.broadcast_to(aim_ref[0], (sub, st))
        lo = lax.broadcasted_iota(jnp.int32, (sub, st), 0) < BATCH
        p_re_ref[...] = jnp.where(lo, ar, ar * ar - ai * ai)
        p_im_ref[...] = jnp.where(lo, ai, 2.0 * ar * ai)
        a1_re_ref[...] = jnp.where(lo, 0.0, ar)
        a1_im_ref[...] = jnp.where(lo, 0.0, ai)

    for b, ub_ref in enumerate((u0_ref, u1_ref, u2_ref, u3_ref)):
        for hh in range(nh):
            us_ref[hh, pl.ds(b, steps, stride=BATCH), :] = ub_ref[:, hh * V7X_LANES:(hh + 1) * V7X_LANES]
    u = jnp.concatenate([us_ref[hh] for hh in range(nh)], axis=1)
    bu_ref[...] = jnp.dot(u.astype(BF16), b_ref[0], preferred_element_type=F32)

    lo8 = lax.broadcasted_iota(jnp.int32, (sub, V7X_LANES), 0) < BATCH

    def step(i, carry):
        r0 = pl.multiple_of(i * sub, sub)
        new = []
        for j in range(nl):
            lre = slice(j * V7X_LANES, (j + 1) * V7X_LANES)
            lim = slice(st + j * V7X_LANES, st + (j + 1) * V7X_LANES)
            xr = bu_ref[pl.ds(r0, sub), lre]
            xi = bu_ref[pl.ds(r0, sub), lim]
            a1r, a1i = a1_re_ref[:, lre], a1_im_ref[:, lre]
            pr, pi = p_re_ref[:, lre], p_im_ref[:, lre]
            sxr = pltpu.roll(xr, BATCH, 0)
            sxi = pltpu.roll(xi, BATCH, 0)
            yr = xr + a1r * sxr - a1i * sxi
            yi = xi + a1r * sxi + a1i * sxr
            hr, hi = carry[2 * j], carry[2 * j + 1]
            hbr = jnp.where(lo8, pltpu.roll(hr, BATCH, 0), hr)
            hbi = jnp.where(lo8, pltpu.roll(hi, BATCH, 0), hi)
            nr = yr + pr * hbr - pi * hbi
            ni = yi + pr * hbi + pi * hbr
            h_ref[pl.ds(r0, sub), lre] = nr
            h_ref[pl.ds(r0, sub), lim] = ni
            new += [nr, ni]
        return tuple(new)

    init = []
    for j in range(nl):
        init += [carry_ref[:, j * V7X_LANES:(j + 1) * V7X_LANES],
                 carry_ref[:, st + j * V7X_LANES:st + (j + 1) * V7X_LANES]]
    final = lax.fori_loop(0, rows // sub, step, tuple(init))
    for j in range(nl):
        carry_ref[:, j * V7X_LANES:(j + 1) * V7X_LANES] = final[2 * j]
        carry_ref[:, st + j * V7X_LANES:st + (j + 1) * V7X_LANES] = final[2 * j + 1]

    y = jnp.dot(h_ref[...].astype(BF16), c_ref[0], preferred_element_type=F32) + d_ref[0] * u
    y = _gelu_tanh(y)
    for hh in range(nh):
        ys_ref[hh] = y[:, hh * V7X_LANES:(hh + 1) * V7X_LANES]
    for b in range(BATCH):
        for hh in range(nh):
            y_ref[b, :, hh * V7X_LANES:(hh + 1) * V7X_LANES] = (
                ys_ref[hh, pl.ds(b, steps, stride=BATCH), :].astype(y_ref.dtype))
    hre_ref[...] = carry_ref[:, 0:st]
    him_ref[...] = carry_ref[:, st:2 * st]


def _ssm_prompt(u, b_mat, c_mat, are, aim, d):
    steps = SSM_STEPS
    rows = steps * BATCH
    nt = SEQ // steps
    st, ch = SSM_TILE_ST, SSM_TILE_CH
    nh = ch // V7X_LANES
    est = (8 * steps * ch * 4 + 2 * BATCH * steps * ch * 2 + 4 * ch * 2 * st * 2 + 2 * rows * 2 * st * 4
           + rows * 2 * st * 2 + 4 * rows * ch * 4 + 8 * 8 * st * 4 + (4 << 20))
    u_spec = lambda b: pl.BlockSpec((steps, ch), lambda g, t, b=b: (b * nt + t, g))
    return pl.pallas_call(
        _ssm_prompt_body,
        out_shape=(jax.ShapeDtypeStruct((BATCH + 1, SEQ, D_MODEL), BF16),
                   jax.ShapeDtypeStruct((V7X_SUBLANES, SSM_GROUPS * SSM_STATE), F32),
                   jax.ShapeDtypeStruct((V7X_SUBLANES, SSM_GROUPS * SSM_STATE), F32)),
        grid=(SSM_TILES, nt),
        in_specs=[u_spec(0), u_spec(1), u_spec(2), u_spec(3),
                  pl.BlockSpec((1, ch, 2 * st), lambda g, t: (g, 0, 0)),
                  pl.BlockSpec((1, 2 * st, ch), lambda g, t: (g, 0, 0)),
                  pl.BlockSpec((1, 1, st), lambda g, t: (g, 0, 0)),
                  pl.BlockSpec((1, 1, st), lambda g, t: (g, 0, 0)),
                  pl.BlockSpec((1, 1, ch), lambda g, t: (g, 0, 0))],
        out_specs=(pl.BlockSpec((BATCH, steps, ch), lambda g, t: (0, t, g)),
                   pl.BlockSpec((V7X_SUBLANES, st), lambda g, t: (0, g)),
                   pl.BlockSpec((V7X_SUBLANES, st), lambda g, t: (0, g))),
        scratch_shapes=[pltpu.VMEM((nh, rows, V7X_LANES), F32), pltpu.VMEM((nh, rows, V7X_LANES), F32),
                        pltpu.VMEM((rows, 2 * st), F32), pltpu.VMEM((rows, 2 * st), F32),
                        pltpu.VMEM((V7X_SUBLANES, 2 * st), F32),
                        pltpu.VMEM((V7X_SUBLANES, st), F32), pltpu.VMEM((V7X_SUBLANES, st), F32),
                        pltpu.VMEM((V7X_SUBLANES, st), F32), pltpu.VMEM((V7X_SUBLANES, st), F32)],
        compiler_params=_cparams(2, est),
        name="ssm_prompt",
    )(u, u, u, u, b_mat, c_mat, are, aim, d)


def _ssm_sample_body(u_ref, h0re_ref, h0im_ref, b_ref, c_ref, are_ref, aim_ref, d_ref,
                     y_ref, hre_ref, him_ref):
    st = SSM_TILE_ST
    u = u_ref[...]
    bu = jnp.dot(u.astype(BF16), b_ref[0], preferred_element_type=F32)
    ar, ai = are_ref[0], aim_ref[0]
    h0r, h0i = h0re_ref[...], h0im_ref[...]
    hr = ar * h0r - ai * h0i + bu[:, 0:st]
    hi = ar * h0i + ai * h0r + bu[:, st:2 * st]
    hre_ref[...] = hr
    him_ref[...] = hi
    hcat = jnp.concatenate([hr, hi], axis=1).astype(BF16)
    y = jnp.dot(hcat, c_ref[0], preferred_element_type=F32) + d_ref[0] * u
    y_ref[...] = _gelu_tanh(y).astype(y_ref.dtype)


def _ssm_sample(u, h0_re, h0_im, b_mat, c_mat, are, aim, d):
    nb = DEC_BATCH
    st, ch = SSM_TILE_ST, SSM_TILE_CH
    est = 2 * (nb * ch * 6 + 4 * nb * st * 4 + 2 * ch * 2 * st * 2) + (4 << 20)
    return pl.pallas_call(
        _ssm_sample_body,
        out_shape=(jax.ShapeDtypeStruct((nb, D_MODEL), BF16),
                   jax.ShapeDtypeStruct((nb, SSM_GROUPS * SSM_STATE), F32),
                   jax.ShapeDtypeStruct((nb, SSM_GROUPS * SSM_STATE), F32)),
        grid=(SSM_TILES,),
        in_specs=[pl.BlockSpec((nb, ch), lambda g: (PROMPT_ROWS // DEC_BATCH, g)),
                  pl.BlockSpec((nb, st), lambda g: (0, g)),
                  pl.BlockSpec((nb, st), lambda g: (0, g)),
                  pl.BlockSpec((1, ch, 2 * st), lambda g: (g, 0, 0)),
                  pl.BlockSpec((1, 2 * st, ch), lambda g: (g, 0, 0)),
                  pl.BlockSpec((1, 1, st), lambda g: (g, 0, 0)),
                  pl.BlockSpec((1, 1, st), lambda g: (g, 0, 0)),
                  pl.BlockSpec((1, 1, ch), lambda g: (g, 0, 0))],
        out_specs=(pl.BlockSpec((nb, ch), lambda g: (0, g)),
                   pl.BlockSpec((nb, st), lambda g: (0, g)),
                   pl.BlockSpec((nb, st), lambda g: (0, g))),
        compiler_params=_cparams(1, est),
        name="ssm_sample",
    )(u, h0_re, h0_im, b_mat, c_mat, are, aim, d)


def kernel(x_prompt, x_sample, cache_swa_k, cache_swa_v, state_conv, state_ssm_re, state_ssm_im, ffn_norm, ffn_w_gu, ffn_w_down, mix_norm, even_w_in, even_q_norm, even_k_norm, even_sinks, even_glu_b, even_conv_w, even_conv_b, even_ln_g, even_ln_b, even_w_out, odd_w_in, odd_lam_re, odd_lam_im, odd_log_dt, odd_b_re, odd_b_im, odd_c_re, odd_c_im, odd_d, odd_w_gate, odd_w_out):
    nb = DEC_BATCH
    ssm = _ssm_params(odd_lam_re[0], odd_lam_im[0], odd_log_dt[0], odd_b_re[0], odd_b_im[0],
                      odd_c_re[0], odd_c_im[0], odd_d[0])

    x, h = _merge_rmsnorm(x_prompt.reshape(PROMPT_ROWS, D_MODEL), x_sample.reshape(nb, D_MODEL),
                          ffn_norm[0, 0])
    x = _half_ffn(x, h, ffn_w_gu, ffn_w_down, (0, 0))

    even_p = (even_sinks[0], even_q_norm[0], even_k_norm[0], even_glu_b[0], even_conv_w[0],
              even_conv_b[0], even_ln_g[0], even_ln_b[0])
    z = _matmul(_rmsnorm(x, mix_norm[0]), even_w_in, (0,), n_cols=EVEN_IN, name="even_in")
    mix, pk, pv, pc = _even_core_prompt(z, *even_p)
    mix_s, sk, sv, sc = _even_core_sample(
        z, cache_swa_k.reshape(nb, WINDOW, KV_COLS), cache_swa_v.reshape(nb, WINDOW, KV_COLS),
        state_conv.reshape(nb, CONV_WIDTH - 1, CONV_CH), *even_p)
    mix = _put_sample_rows(mix, mix_s.reshape(nb, D_MODEL))
    x = _matmul(mix, even_w_out, (0,), n_cols=D_MODEL, epilogue="res", extra=x, name="even_out")
    x = _half_ffn(x, _rmsnorm(x, ffn_norm[0, 1]), ffn_w_gu, ffn_w_down, (0, 1))

    x = _half_ffn(x, _rmsnorm(x, ffn_norm[1, 0]), ffn_w_gu, ffn_w_down, (1, 0))
    u = _matmul(_rmsnorm(x, mix_norm[1]), odd_w_in, (0,), n_cols=D_MODEL, name="odd_in")
    y3, pre8, pim8 = _ssm_prompt(u, *ssm)
    y_s, sre, sim = _ssm_sample(u, state_ssm_re.reshape(nb, -1), state_ssm_im.reshape(nb, -1), *ssm)
    y = _put_sample_rows(y3.reshape((BATCH + 1) * SEQ, D_MODEL), y_s)
    yg = _matmul(y, odd_w_gate, (0,), n_cols=D_MODEL, epilogue="gate", out_dtype=BF16, extra=y,
                 name="odd_gate")
    x = _matmul(yg, odd_w_out, (0,), n_cols=D_MODEL, epilogue="res", extra=x, name="odd_out")
    y_p, y_smp = _half_ffn(x, _rmsnorm(x, ffn_norm[1, 1]), ffn_w_gu, ffn_w_down, (1, 1), split_out=True)

    kv5 = lambda t, b: t.reshape(1, b, WINDOW, N_KV_HEADS, HEAD_DIM)
    st4 = lambda t, b: t.reshape(1, b, SSM_GROUPS, SSM_STATE)
    return (y_p.reshape(BATCH, SEQ, D_MODEL), y_smp.reshape(nb, 1, D_MODEL),
            kv5(pk, BATCH), kv5(pv, BATCH), pc.reshape(1, BATCH, CONV_WIDTH - 1, CONV_CH),
            st4(pre8[BATCH:2 * BATCH], BATCH), st4(pim8[BATCH:2 * BATCH], BATCH),
            kv5(sk, nb), kv5(sv, nb), sc.reshape(1, nb, CONV_WIDTH - 1, CONV_CH),
            st4(sre, nb), st4(sim, nb))
```

```python
import functools
import math

import jax
import jax.numpy as jnp
from jax import lax
from jax.experimental import pallas as pl
from jax.experimental.pallas import tpu as pltpu

F32 = jnp.float32
BF16 = jnp.bfloat16

D_MODEL = 4096
BATCH = 4
SEQ = 2048
DEC_BATCH = 32
PAST_LEN = 16384
HEAD_DIM = 128
N_HEADS = 16
N_KV_HEADS = 4
Q_PER_KV = 4
WINDOW = 128
ROT_DIM = 32
ROPE_THETA = 500000.0
ATT_WIDTH = 2048
CONV_CH = 2048
CONV_WIDTH = 31
Q_COLS = 2048
KV_COLS = 512
EVEN_IN = 7168
SSM_GROUPS = 256
SSM_GROUP = 16
SSM_STATE = 64
D_FF = 11008
EPS = 1e-6
NEG_INF = -1e30

V7X_VMEM_BYTES = 64 * 1024 * 1024
V7X_LANES = 128
V7X_SUBLANES = 8
VMEM_CAP = V7X_VMEM_BYTES - 6 * 1024 * 1024

PROMPT_ROWS = BATCH * SEQ
MERGE_TILE = 128
M_ROWS = PROMPT_ROWS + MERGE_TILE
ROW_TILES = 8
TM = M_ROWS // ROW_TILES
TM_DOWN = TM // 2

SSM_TILE_GROUPS = 16
SSM_TILE_CH = SSM_TILE_GROUPS * SSM_GROUP
SSM_TILE_ST = SSM_TILE_GROUPS * SSM_STATE
SSM_TILES = SSM_GROUPS // SSM_TILE_GROUPS
SSM_STEPS = 256
CONV_HALO = 32


def _cparams(n_grid, est_bytes):
    limit = int(min(VMEM_CAP, est_bytes * 5 // 4 + (4 << 20)))
    return pltpu.CompilerParams(dimension_semantics=("arbitrary",) * n_grid,
                                vmem_limit_bytes=limit)


def _sigmoid(x):
    return 0.5 * (1.0 + jnp.tanh(0.5 * x))


def _rms(x, g):
    ms = jnp.mean(x * x, axis=-1, keepdims=True)
    return x * lax.rsqrt(ms + EPS) * g


def _rmsnorm_body(x_ref, g_ref, o_ref):
    o_ref[...] = _rms(x_ref[...], g_ref[...]).astype(o_ref.dtype)


def _rmsnorm(x, g):
    d = x.shape[1]
    tr = TM // 5
    return pl.pallas_call(
        _rmsnorm_body,
        out_shape=jax.ShapeDtypeStruct((M_ROWS, d), BF16),
        grid=(M_ROWS // tr,),
        in_specs=[pl.BlockSpec((tr, d), lambda i: (i, 0)),
                  pl.BlockSpec((1, d), lambda i: (0, 0))],
        out_specs=pl.BlockSpec((tr, d), lambda i: (i, 0)),
        compiler_params=_cparams(1, 2 * tr * d * 4 + 2 * tr * d * 2),
        name="rmsnorm",
    )(x, g.reshape(1, d))


def _merge_rmsnorm_body(xp_ref, xs_ref, g_ref, x_ref, h_ref):
    i = pl.program_id(0)
    g = g_ref[...]

    @pl.when(i < PROMPT_ROWS // MERGE_TILE)
    def _prompt():
        x = xp_ref[...]
        x_ref[...] = x
        h_ref[...] = _rms(x, g).astype(h_ref.dtype)

    @pl.when(i == PROMPT_ROWS // MERGE_TILE)
    def _sample():
        xs = xs_ref[...]
        pad = MERGE_TILE - DEC_BATCH
        x_ref[0:DEC_BATCH, :] = xs
        x_ref[DEC_BATCH:MERGE_TILE, :] = jnp.zeros((pad, D_MODEL), F32)
        h_ref[0:DEC_BATCH, :] = _rms(xs, g).astype(h_ref.dtype)
        h_ref[DEC_BATCH:MERGE_TILE, :] = jnp.zeros((pad, D_MODEL), h_ref.dtype)


def _merge_rmsnorm(xp, xs, g):
    d = D_MODEL
    last = PROMPT_ROWS // MERGE_TILE - 1
    return pl.pallas_call(
        _merge_rmsnorm_body,
        out_shape=(jax.ShapeDtypeStruct((M_ROWS, d), F32), jax.ShapeDtypeStruct((M_ROWS, d), BF16)),
        grid=(M_ROWS // MERGE_TILE,),
        in_specs=[pl.BlockSpec((MERGE_TILE, d), lambda i: (jnp.minimum(i, last), 0)),
                  pl.BlockSpec((DEC_BATCH, d), lambda i: (0, 0)),
                  pl.BlockSpec((1, d), lambda i: (0, 0))],
        out_specs=(pl.BlockSpec((MERGE_TILE, d), lambda i: (i, 0)),
                   pl.BlockSpec((MERGE_TILE, d), lambda i: (i, 0))),
        compiler_params=_cparams(1, 4 * MERGE_TILE * d * 4 + 2 * MERGE_TILE * d * 2 + 2 * DEC_BATCH * d * 4),
        name="merge_rmsnorm",
    )(xp, xs, g.reshape(1, d))


def _mm_body(*refs, n_w, cast_w, epilogue, scale, side_cast, split_rows):
    a_ref = refs[0]
    w_refs = refs[1:1 + n_w]
    pos = 1 + n_w
    extra_ref = side_in_ref = side_out_ref = o2_ref = None
    if epilogue in ("res", "gate"):
        extra_ref = refs[pos]
        pos += 1
    if side_cast:
        side_in_ref = refs[pos]
        pos += 1
    o_ref = refs[pos]
    pos += 1
    if side_cast:
        side_out_ref = refs[pos]
        pos += 1
    if split_rows is not None:
        o2_ref = refs[pos]
        pos += 1
    tn = w_refs[0].shape[-1]
    if cast_w:
        wb_ref = refs[pos]

        @pl.when(pl.program_id(1) == 0)
        def _cast_weights():
            for n, w_ref in enumerate(w_refs):
                wb_ref[:, n * tn:(n + 1) * tn] = w_ref[...].astype(BF16)
    else:
        wb_ref = w_refs[0]

    if side_cast:
        side_out_ref[...] = side_in_ref[...].astype(side_out_ref.dtype)

    acc = jnp.dot(a_ref[...].astype(BF16), wb_ref[...], preferred_element_type=F32)
    if epilogue == "plain":
        out = acc
    elif epilogue == "swiglu":
        gate, up = acc[:, 0:tn], acc[:, tn:2 * tn]
        out = gate * _sigmoid(gate) * up
    elif epilogue == "res":
        out = extra_ref[...] + scale * acc
    elif epilogue == "gate":
        out = extra_ref[...].astype(F32) * _sigmoid(acc)
    o_ref[...] = out.astype(o_ref.dtype)

    if split_rows is not None:
        @pl.when(pl.program_id(1) == pl.num_programs(1) - 1)
        def _sample_rows():
            o2_ref[...] = out[split_rows:split_rows + DEC_BATCH, :].astype(o2_ref.dtype)


def _matmul(a, w, widx=(), *, n_cols, tm=TM, tn=512, epilogue="plain", out_dtype=F32, col_offs=(0,),
            extra=None, scale=1.0, side=None, split_out=False, name="mm"):
    k = a.shape[1]
    n_w = len(col_offs)
    cast_w = w.dtype != BF16
    nlead = len(widx)
    n_i = M_ROWS // tm
    n_j = n_cols // tn

    operands = [a]
    in_specs = [pl.BlockSpec((tm, k), lambda j, i: (i, 0))]
    for off in col_offs:
        operands.append(w)
        in_specs.append(pl.BlockSpec((None,) * nlead + (k, tn),
                                     lambda j, i, off=off: tuple(widx) + (0, j + off)))
    w_bytes = w.dtype.itemsize
    est = 2 * tm * k * a.dtype.itemsize + n_w * (2 * k * tn * w_bytes + (k * tn * 2 if cast_w else 0))
    if extra is not None:
        operands.append(extra)
        in_specs.append(pl.BlockSpec((tm, tn), lambda j, i: (i, j)))
        est += 2 * tm * tn * extra.dtype.itemsize
    out_shapes, out_specs = [], []
    if split_out:
        out_shapes.append(jax.ShapeDtypeStruct((PROMPT_ROWS, n_cols), out_dtype))
    else:
        out_shapes.append(jax.ShapeDtypeStruct((M_ROWS, n_cols), out_dtype))
    out_specs.append(pl.BlockSpec((tm, tn), lambda j, i: (i, j)))
    if side is not None:
        w_src, widx_src, rows_total = side
        slab = rows_total // (n_i * n_j)
        assert slab * n_i * n_j == rows_total and slab % 16 == 0
        cols_src = w_src.shape[-1]
        operands.append(w_src)
        in_specs.append(pl.BlockSpec((None,) * len(widx_src) + (slab, cols_src),
                                     lambda j, i: tuple(widx_src) + (j * n_i + i, 0)))
        out_shapes.append(jax.ShapeDtypeStruct((rows_total, cols_src), BF16))
        out_specs.append(pl.BlockSpec((slab, cols_src), lambda j, i: (j * n_i + i, 0)))
        est += 2 * slab * cols_src * 6
    split_rows = None
    if split_out:
        split_rows = PROMPT_ROWS - (n_i - 1) * tm
        assert split_rows % V7X_SUBLANES == 0 and 0 <= split_rows <= tm - DEC_BATCH
        out_shapes.append(jax.ShapeDtypeStruct((DEC_BATCH, n_cols), out_dtype))
        out_specs.append(pl.BlockSpec((DEC_BATCH, tn), lambda j, i: (0, j)))
    est += 2 * tm * tn * jnp.dtype(out_dtype).itemsize + (n_w + 1) * tm * tn * 4

    outs = pl.pallas_call(
        functools.partial(_mm_body, n_w=n_w, cast_w=cast_w, epilogue=epilogue, scale=scale,
                          side_cast=side is not None, split_rows=split_rows),
        out_shape=tuple(out_shapes),
        grid=(n_j, n_i),
        in_specs=in_specs,
        out_specs=tuple(out_specs),
        scratch_shapes=[pltpu.VMEM((k, n_w * tn), BF16)] if cast_w else [],
        compiler_params=_cparams(2, est),
        name=name,
    )(*operands)
    return outs[0] if len(outs) == 1 else outs


def _half_ffn(x, h, w_gu, w_down, widx, *, split_out=False):
    act, wd_bf16 = _matmul(h, w_gu, widx, n_cols=D_FF, tn=256, epilogue="swiglu", out_dtype=BF16,
                           col_offs=(0, D_FF // 256), side=(w_down, widx, D_FF), name="ffn_gu")
    return _matmul(act, wd_bf16, n_cols=D_MODEL, tm=TM_DOWN, tn=512, epilogue="res", extra=x, scale=0.5,
                   split_out=split_out, name="ffn_down")


def _rope_tables(pos):
    half = ROT_DIM // 2
    inv_freq = jnp.power(jnp.float32(ROPE_THETA), -jnp.arange(half, dtype=F32) * (2.0 / ROT_DIM))
    ang = pos.astype(F32)[:, None] * inv_freq[None, :]
    cos, sin = jnp.cos(ang), jnp.sin(ang)
    n = pos.shape[0]
    rest = HEAD_DIM - ROT_DIM
    c = jnp.concatenate([cos, cos, jnp.ones((n, rest), F32)], axis=1)
    sp = jnp.concatenate([jnp.zeros((n, half), F32), sin, jnp.zeros((n, rest), F32)], axis=1)
    sm = jnp.concatenate([-sin, jnp.zeros((n, HEAD_DIM - half), F32)], axis=1)
    return c, sp, sm


def _norm_rope(x, g, c, sp, sm):
    half = ROT_DIM // 2
    y = x * lax.rsqrt(jnp.mean(x * x, axis=-1, keepdims=True) + EPS) * g
    return y * c + pltpu.roll(y, half, 1) * sp + pltpu.roll(y, HEAD_DIM - half, 1) * sm


def _layer_norm_silu(c, g, b):
    mu = jnp.mean(c, axis=-1, keepdims=True)
    xc = c - mu
    var = jnp.mean(xc * xc, axis=-1, keepdims=True)
    y = xc * lax.rsqrt(var + EPS) * g + b
    return y * _sigmoid(y)


def _even_prompt_body(sinks_ref, q_ref, kv_ref, a0_ref, a1_ref, g0_ref, g1_ref,
                      c_ref, sp_ref, sm_ref, qn_ref, kn_ref, glub_ref, cw_ref, cb_ref,
                      lng_ref, lnb_ref,
                      mix_ref, nk_ref, nv_ref, nc_ref,
                      kprev_ref, vprev_ref, gbuf_ref, cbuf_ref):
    blk = WINDOW
    i = pl.program_id(1)

    @pl.when(i == 0)
    def _reset():
        kprev_ref[...] = jnp.zeros_like(kprev_ref)
        vprev_ref[...] = jnp.zeros_like(vprev_ref)
        gbuf_ref[0:CONV_HALO, :] = jnp.zeros((CONV_HALO, CONV_CH), F32)

    c, sp, sm = c_ref[...], sp_ref[...], sm_ref[...]
    qn, kn = qn_ref[...], kn_ref[...]

    row = lax.broadcasted_iota(jnp.int32, (blk, 2 * blk), 0)
    col = lax.broadcasted_iota(jnp.int32, (blk, 2 * blk), 1)
    first_key = jnp.where(i > 0, 0, blk)
    mask = (col >= jnp.maximum(row, first_key)) & (col <= row + WINDOW)
    scale = HEAD_DIM ** -0.5
    for g in range(N_KV_HEADS):
        ks = slice(g * HEAD_DIM, (g + 1) * HEAD_DIM)
        k_cur = _norm_rope(kv_ref[:, ks], kn, c, sp, sm)
        v_cur = kv_ref[:, KV_COLS + g * HEAD_DIM:KV_COLS + (g + 1) * HEAD_DIM]
        nk_ref[0, :, ks] = k_cur
        nv_ref[0, :, ks] = v_cur
        k_cur_b = k_cur.astype(BF16)
        v_cur_b = v_cur.astype(BF16)
        k2 = jnp.concatenate([kprev_ref[:, ks], k_cur_b], axis=0)
        v2 = jnp.concatenate([vprev_ref[:, ks], v_cur_b], axis=0)
        for r in range(Q_PER_KV):
            h = g * Q_PER_KV + r
            hs = slice(h * HEAD_DIM, (h + 1) * HEAD_DIM)
            qh = _norm_rope(q_ref[:, hs], qn, c, sp, sm).astype(BF16)
            s = lax.dot_general(qh, k2, (((1,), (1,)), ((), ())),
                                preferred_element_type=F32) * scale
            s = jnp.where(mask, s, NEG_INF)
            sink = sinks_ref[h]
            m = jnp.maximum(jnp.max(s, axis=-1, keepdims=True), sink)
            p = jnp.exp(s - m)
            denom = jnp.sum(p, axis=-1, keepdims=True) + jnp.exp(sink - m)
            w = (p / denom).astype(BF16)
            o = jnp.dot(w, v2, preferred_element_type=F32)
            mix_ref[:, hs] = o.astype(mix_ref.dtype)
        kprev_ref[:, ks] = k_cur_b
        vprev_ref[:, ks] = v_cur_b

    half_ch = CONV_CH // 2
    for hh, (a_ref, g_ref) in enumerate(((a0_ref, g0_ref), (a1_ref, g1_ref))):
        cs = slice(hh * half_ch, (hh + 1) * half_ch)
        a = a_ref[...] + glub_ref[:, cs]
        gate = g_ref[...] + glub_ref[:, CONV_CH + hh * half_ch:CONV_CH + (hh + 1) * half_ch]
        gbuf_ref[CONV_HALO:CONV_HALO + blk, cs] = a * _sigmoid(gate)

    lane_chunk = V7X_LANES
    first = CONV_HALO - (CONV_WIDTH - 1)
    sub = V7X_SUBLANES

    def conv_chunk(ci, carry):
        ls = pl.ds(pl.multiple_of(ci * lane_chunk, lane_chunk), lane_chunk)
        acc = jnp.zeros((blk, lane_chunk), F32)
        for s in range(sub):
            taps = [w for w in range(CONV_WIDTH) if (first + w) % sub == s]
            rows = blk if s == 0 else blk + sub
            part = jnp.zeros((rows, lane_chunk), F32)
            for w in taps:
                base = first + w - s
                part = part + cw_ref[w:w + 1, ls] * gbuf_ref[base:base + rows, ls]
            acc = acc + part[s:s + blk, :]
        cbuf_ref[:, ls] = acc
        return carry

    lax.fori_loop(0, CONV_CH // lane_chunk, conv_chunk, 0)
    y = _layer_norm_silu(cbuf_ref[...] + cb_ref[...], lng_ref[...], lnb_ref[...])
    mix_ref[:, ATT_WIDTH:ATT_WIDTH + CONV_CH] = y.astype(mix_ref.dtype)

    @pl.when(i == pl.num_programs(1) - 1)
    def _emit_conv_state():
        nc_ref[0] = gbuf_ref[CONV_HALO + blk - (CONV_WIDTH - 1):CONV_HALO + blk, :]

    gbuf_ref[0:CONV_HALO, :] = gbuf_ref[blk:blk + CONV_HALO, :]


def _even_core_prompt(z, sinks, qn, kn, glub, cw, cb, lng, lnb):
    nb = SEQ // WINDOW
    blk = WINDOW
    c, sp, sm = _rope_tables(jnp.arange(SEQ, dtype=jnp.int32))
    rowmap = lambda col: (lambda b, i: (b * nb + i, col))
    tab = pl.BlockSpec((blk, HEAD_DIM), lambda b, i: (i, 0))
    vec = lambda n: pl.BlockSpec((1, n), lambda b, i: (0, 0))
    est = (2 * blk * (2048 + 1024 + 4 * 1024) * 4 + 2 * blk * 4096 * 2 + 4 * blk * 512 * 4
           + (blk + CONV_HALO) * CONV_CH * 4 + 3 * blk * CONV_CH * 4 + 2 * CONV_WIDTH * CONV_CH * 4 + (8 << 20))
    return pl.pallas_call(
        _even_prompt_body,
        out_shape=(jax.ShapeDtypeStruct((M_ROWS, 2 * ATT_WIDTH), BF16),
                   jax.ShapeDtypeStruct((BATCH, blk, KV_COLS), F32),
                   jax.ShapeDtypeStruct((BATCH, blk, KV_COLS), F32),
                   jax.ShapeDtypeStruct((BATCH, CONV_WIDTH - 1, CONV_CH), F32)),
        grid=(BATCH, nb),
        in_specs=[pl.BlockSpec(memory_space=pltpu.SMEM),
                  pl.BlockSpec((blk, 2048), rowmap(0)),
                  pl.BlockSpec((blk, 1024), rowmap(2)),
                  pl.BlockSpec((blk, 1024), rowmap(3)),
                  pl.BlockSpec((blk, 1024), rowmap(4)),
                  pl.BlockSpec((blk, 1024), rowmap(5)),
                  pl.BlockSpec((blk, 1024), rowmap(6)),
                  tab, tab, tab, vec(HEAD_DIM), vec(HEAD_DIM), vec(2 * CONV_CH),
                  pl.BlockSpec((CONV_WIDTH, CONV_CH), lambda b, i: (0, 0)),
                  vec(CONV_CH), vec(CONV_CH), vec(CONV_CH)],
        out_specs=(pl.BlockSpec((blk, 2 * ATT_WIDTH), rowmap(0)),
                   pl.BlockSpec((1, blk, KV_COLS), lambda b, i: (b, 0, 0)),
                   pl.BlockSpec((1, blk, KV_COLS), lambda b, i: (b, 0, 0)),
                   pl.BlockSpec((1, CONV_WIDTH - 1, CONV_CH), lambda b, i: (b, 0, 0))),
        scratch_shapes=[pltpu.VMEM((blk, KV_COLS), BF16), pltpu.VMEM((blk, KV_COLS), BF16),
                        pltpu.VMEM((blk + CONV_HALO, CONV_CH), F32), pltpu.VMEM((blk, CONV_CH), F32)],
        compiler_params=_cparams(2, est),
        name="even_core_prompt",
    )(sinks, z, z, z, z, z, z, c, sp, sm, qn.reshape(1, -1), kn.reshape(1, -1), glub.reshape(1, -1),
      cw, cb.reshape(1, -1), lng.reshape(1, -1), lnb.reshape(1, -1))


def _even_sample_body(sinks_ref, z_ref, ck_ref, cv_ref, sc_ref, c_ref, sp_ref, sm_ref, qn_ref, kn_ref,
                      glub_ref, cw_ref, cb_ref, lng_ref, lnb_ref,
                      mix_ref, nk_ref, nv_ref, nc_ref):
    zrow = lambda lo, hi: z_ref[0, :, lo:hi]
    c, sp, sm = c_ref[...], sp_ref[...], sm_ref[...]
    qn, kn = qn_ref[...], kn_ref[...]
    nbuf = WINDOW
    scale = HEAD_DIM ** -0.5
    qrow = lax.broadcasted_iota(jnp.int32, (V7X_SUBLANES, 1), 0)
    krow = lax.broadcasted_iota(jnp.int32, (nbuf, HEAD_DIM), 0)
    for g in range(N_KV_HEADS):
        ks = slice(g * HEAD_DIM, (g + 1) * HEAD_DIM)
        k_new = _norm_rope(zrow(Q_COLS + g * HEAD_DIM, Q_COLS + (g + 1) * HEAD_DIM), kn, c, sp, sm)
        v_new = zrow(Q_COLS + KV_COLS + g * HEAD_DIM, Q_COLS + KV_COLS + (g + 1) * HEAD_DIM)
        k_old = ck_ref[0, :, ks]
        v_old = cv_ref[0, :, ks]
        nk_ref[0, :, ks] = jnp.where(krow == nbuf - 1, k_new, pltpu.roll(k_old, nbuf - 1, 0))
        nv_ref[0, :, ks] = jnp.where(krow == nbuf - 1, v_new, pltpu.roll(v_old, nbuf - 1, 0))
        qs = [_norm_rope(zrow((g * Q_PER_KV + r) * HEAD_DIM, (g * Q_PER_KV + r + 1) * HEAD_DIM),
                         qn, c, sp, sm) for r in range(Q_PER_KV)]
        q8 = jnp.zeros((V7X_SUBLANES, HEAD_DIM), F32)
        for r in range(Q_PER_KV):
            q8 = jnp.where(qrow == r, qs[r], q8)
        q8b = q8.astype(BF16)
        s = lax.dot_general(q8b, k_old.astype(BF16), (((1,), (1,)), ((), ())),
                            preferred_element_type=F32) * scale
        k_new_r = k_new.astype(BF16).astype(F32)
        v_new_r = v_new.astype(BF16).astype(F32)
        s_self = jnp.sum(q8b.astype(F32) * k_new_r, axis=-1, keepdims=True) * scale
        sink = jnp.zeros((V7X_SUBLANES, 1), F32)
        for r in range(Q_PER_KV):
            sink = jnp.where(qrow == r, sinks_ref[g * Q_PER_KV + r], sink)
        m = jnp.maximum(jnp.maximum(jnp.max(s, axis=-1, keepdims=True), s_self), sink)
        p = jnp.exp(s - m)
        p_self = jnp.exp(s_self - m)
        denom = jnp.sum(p, axis=-1, keepdims=True) + p_self + jnp.exp(sink - m)
        o = jnp.dot((p / denom).astype(BF16), v_old.astype(BF16), preferred_element_type=F32)
        o = o + (p_self / denom).astype(BF16).astype(F32) * v_new_r
        for r in range(Q_PER_KV):
            h = g * Q_PER_KV + r
            mix_ref[0, :, h * HEAD_DIM:(h + 1) * HEAD_DIM] = o[r:r + 1, :]

    conv0 = Q_COLS + 2 * KV_COLS
    a = zrow(conv0, conv0 + CONV_CH) + glub_ref[:, 0:CONV_CH]
    gate = zrow(conv0 + CONV_CH, EVEN_IN) + glub_ref[:, CONV_CH:2 * CONV_CH]
    gl = a * _sigmoid(gate)
    nst = CONV_WIDTH - 1
    conv = jnp.sum(cw_ref[0:nst, :] * sc_ref[0], axis=0, keepdims=True) + cw_ref[nst:nst + 1, :] * gl
    y = _layer_norm_silu(conv + cb_ref[...], lng_ref[...], lnb_ref[...])
    mix_ref[0, :, ATT_WIDTH:ATT_WIDTH + CONV_CH] = y
    nc_ref[0, 0:nst - 1, :] = sc_ref[0, 1:nst, :]
    nc_ref[0, nst - 1:nst, :] = gl


def _even_core_sample(z, cache_k, cache_v, state_conv, sinks, qn, kn, glub, cw, cb, lng, lnb):
    nb = DEC_BATCH
    c, sp, sm = _rope_tables(jnp.full((1,), PAST_LEN, jnp.int32))
    vec = lambda n: pl.BlockSpec((1, n), lambda b: (0, 0))
    per_b = lambda r, n: pl.BlockSpec((1, r, n), lambda b: (b, 0, 0))
    nst = CONV_WIDTH - 1
    est = 2 * (EVEN_IN * 4 * nb + 4 * WINDOW * KV_COLS * 4 + 2 * 32 * CONV_CH * 4 + 4096 * 4 * 8) + (8 << 20)
    return pl.pallas_call(
        _even_sample_body,
        out_shape=(jax.ShapeDtypeStruct((nb, 1, 2 * ATT_WIDTH), F32),
                   jax.ShapeDtypeStruct((nb, WINDOW, KV_COLS), F32),
                   jax.ShapeDtypeStruct((nb, WINDOW, KV_COLS), F32),
                   jax.ShapeDtypeStruct((nb, nst, CONV_CH), F32)),
        grid=(nb,),
        in_specs=[pl.BlockSpec(memory_space=pltpu.SMEM),
                  per_b(1, EVEN_IN),
                  per_b(WINDOW, KV_COLS), per_b(WINDOW, KV_COLS), per_b(nst, CONV_CH),
                  vec(HEAD_DIM), vec(HEAD_DIM), vec(HEAD_DIM), vec(HEAD_DIM), vec(HEAD_DIM),
                  vec(2 * CONV_CH), pl.BlockSpec((CONV_WIDTH, CONV_CH), lambda b: (0, 0)),
                  vec(CONV_CH), vec(CONV_CH), vec(CONV_CH)],
        out_specs=(per_b(1, 2 * ATT_WIDTH), per_b(WINDOW, KV_COLS), per_b(WINDOW, KV_COLS),
                   per_b(nst, CONV_CH)),
        compiler_params=_cparams(1, est),
        name="even_core_sample",
    )(sinks, z[PROMPT_ROWS:PROMPT_ROWS + nb].reshape(nb, 1, EVEN_IN), cache_k, cache_v, state_conv, c, sp, sm,
      qn.reshape(1, -1), kn.reshape(1, -1), glub.reshape(1, -1), cw, cb.reshape(1, -1),
      lng.reshape(1, -1), lnb.reshape(1, -1))


def _put_sample_rows(merged, sample_rows):
    pad = jnp.zeros((MERGE_TILE - DEC_BATCH, merged.shape[1]), merged.dtype)
    tile = jnp.concatenate([sample_rows.astype(merged.dtype), pad], axis=0)
    return lax.dynamic_update_slice(merged, tile, (PROMPT_ROWS, 0))


def _ssm_prep_body(lre_ref, lim_ref, ldt_ref, btre_ref, btim_ref,
                   are_ref, aim_ref, bbre_ref, bbim_ref, abre_ref, abim_ref, cre_ref, cim_ref):
    lre, lim = lre_ref[...], lim_ref[...]
    dt = jnp.exp(ldt_ref[...])
    mag = jnp.exp(lre * dt)
    ang = lim * dt
    are = mag * jnp.cos(ang)
    aim = mag * jnp.sin(ang)
    are_ref[...] = are
    aim_ref[...] = aim
    nre, nim = are - 1.0, aim
    den = lre * lre + lim * lim
    cre_ref[...] = (nre * lre + nim * lim) / den
    cim_ref[...] = (nim * lre - nre * lim) / den

    def per_group(g, carry):
        cr = cre_ref[pl.ds(g, 1), :]
        ci = cim_ref[pl.ds(g, 1), :]
        br, bi = btre_ref[g], btim_ref[g]
        bbr = cr * br - ci * bi
        bbi = cr * bi + ci * br
        bbre_ref[g] = bbr
        bbim_ref[g] = bbi
        ar = are_ref[pl.ds(g, 1), :]
        ai = aim_ref[pl.ds(g, 1), :]
        abre_ref[g] = ar * bbr - ai * bbi
        abim_ref[g] = ar * bbi + ai * bbr
        return carry

    lax.fori_loop(0, SSM_GROUPS, per_group, 0)


def _ssm_params(lam_re, lam_im, log_dt, b_re, b_im, c_re, c_im, d_skip):
    g, n, p = SSM_GROUPS, SSM_STATE, SSM_GROUP
    bt_re = jnp.swapaxes(b_re, 1, 2)
    bt_im = jnp.swapaxes(b_im, 1, 2)
    gn = jax.ShapeDtypeStruct((g, n), F32)
    gpn = jax.ShapeDtypeStruct((g, p, n), F32)
    are, aim, bbre, bbim, abre, abim = pl.pallas_call(
        _ssm_prep_body,
        out_shape=(gn, gn, gpn, gpn, gpn, gpn),
        scratch_shapes=[pltpu.VMEM((g, n), F32), pltpu.VMEM((g, n), F32)],
        name="ssm_prep",
    )(lam_re, lam_im, log_dt.reshape(g, 1), bt_re, bt_im)

    t, tg = SSM_TILES, SSM_TILE_GROUPS
    eye = jnp.eye(tg, dtype=BF16)

    def bmat(bb):
        b5 = bb.astype(BF16).reshape(t, tg, p, 1, n) * eye[None, :, None, :, None]
        return b5.reshape(t, tg * p, tg * n)

    def cmat(cc):
        c4 = jnp.transpose(cc.astype(BF16).reshape(t, tg, p, n), (0, 1, 3, 2))
        c5 = c4[:, :, :, None, :] * eye[None, :, None, :, None]
        return c5.reshape(t, tg * n, tg * p)

    b_mat = jnp.concatenate([jnp.concatenate([bmat(bbre), bmat(bbim)], axis=2),
                             jnp.concatenate([bmat(abre), bmat(abim)], axis=2)], axis=1)
    c_mat = jnp.concatenate([cmat(c_re), -cmat(c_im)], axis=1)
    return (b_mat, c_mat, are.reshape(t, 1, tg * n), aim.reshape(t, 1, tg * n),
            d_skip.reshape(t, 1, tg * p))


def _gelu_tanh(y):
    return 0.5 * y * (1.0 + jnp.tanh(math.sqrt(2.0 / math.pi) * (y + 0.044715 * (y * y * y))))


def _ssm_prompt_body(u0_ref, u1_ref, u2_ref, u3_ref, b_ref, c_ref, are_ref, aim_ref, d_ref,
                     y_ref, hre_ref, him_ref,
                     us_ref, ys_ref, bu_ref, h_ref, carry_ref, p_re_ref, p_im_ref):
    st = SSM_TILE_ST
    sub = V7X_SUBLANES
    nl = st // V7X_LANES
    nh = SSM_TILE_CH // V7X_LANES
    tt = pl.program_id(1)
    steps = u0_ref.shape[0]
    rows = steps * BATCH

    @pl.when(tt == 0)
    def _init():
        carry_ref[...] = jnp.zeros_like(carry_ref)
        ar = jnp.broadcast_to(are_ref[0], (sub, st))
        ai = jnp.broadcast_to(aim_ref[0], (sub, st))
        lo = lax.broadcasted_iota(jnp.int32, (sub, st), 0) < BATCH
        p_re_ref[...] = jnp.where(lo, ar, ar * ar - ai * ai)
        p_im_ref[...] = jnp.where(lo, ai, 2.0 * ar * ai)

    for b, ub_ref in enumerate((u0_ref, u1_ref, u2_ref, u3_ref)):
        for hh in range(nh):
            us_ref[hh, pl.ds(b, steps, stride=BATCH), :] = ub_ref[:, hh * V7X_LANES:(hh + 1) * V7X_LANES]
    u = jnp.concatenate([us_ref[hh] for hh in range(nh)], axis=1)
    second = (lax.broadcasted_iota(jnp.int32, u.shape, 0) % sub) >= BATCH
    u_prev = jnp.where(second, pltpu.roll(u, BATCH, 0), 0.0)
    u2 = jnp.concatenate([u, u_prev], axis=1).astype(BF16)
    bu_ref[...] = jnp.dot(u2, b_ref[0], preferred_element_type=F32)

    lo8 = lax.broadcasted_iota(jnp.int32, (sub, V7X_LANES), 0) < BATCH

    pack = 2 * sub

    def step(i, carry):
        r0 = pl.multiple_of(i * pack, pack)
        new = []
        for j in range(nl):
            lre = slice(j * V7X_LANES, (j + 1) * V7X_LANES)
            lim = slice(st + j * V7X_LANES, st + (j + 1) * V7X_LANES)
            pr, pi = p_re_ref[:, lre], p_im_ref[:, lre]
            hr, hi = carry[2 * j], carry[2 * j + 1]
            outs_r, outs_i = [], []
            for k in range(2):
                yr = bu_ref[pl.ds(r0 + k * sub, sub), lre]
                yi = bu_ref[pl.ds(r0 + k * sub, sub), lim]
                hbr = jnp.where(lo8, pltpu.roll(hr, BATCH, 0), hr)
                hbi = jnp.where(lo8, pltpu.roll(hi, BATCH, 0), hi)
                hr = yr + pr * hbr - pi * hbi
                hi = yi + pr * hbi + pi * hbr
                outs_r.append(hr)
                outs_i.append(hi)
            h_ref[pl.ds(r0, pack), lre] = jnp.concatenate(outs_r, axis=0).astype(h_ref.dtype)
            h_ref[pl.ds(r0, pack), lim] = jnp.concatenate(outs_i, axis=0).astype(h_ref.dtype)
            new += [hr, hi]
        return tuple(new)

    init = []
    for j in range(nl):
        init += [carry_ref[:, j * V7X_LANES:(j + 1) * V7X_LANES],
                 carry_ref[:, st + j * V7X_LANES:st + (j + 1) * V7X_LANES]]
    final = lax.fori_loop(0, rows // pack, step, tuple(init))
    for j in range(nl):
        carry_ref[:, j * V7X_LANES:(j + 1) * V7X_LANES] = final[2 * j]
        carry_ref[:, st + j * V7X_LANES:st + (j + 1) * V7X_LANES] = final[2 * j + 1]

    y = jnp.dot(h_ref[...], c_ref[0], preferred_element_type=F32) + d_ref[0] * u
    y = _gelu_tanh(y)
    for hh in range(nh):
        ys_ref[hh] = y[:, hh * V7X_LANES:(hh + 1) * V7X_LANES]
    for b in range(BATCH):
        for hh in range(nh):
            y_ref[b, :, hh * V7X_LANES:(hh + 1) * V7X_LANES] = (
                ys_ref[hh, pl.ds(b, steps, stride=BATCH), :].astype(y_ref.dtype))
    hre_ref[...] = carry_ref[:, 0:st]
    him_ref[...] = carry_ref[:, st:2 * st]


def _ssm_prompt(u, b_mat, c_mat, are, aim, d):
    steps = SSM_STEPS
    rows = steps * BATCH
    nt = SEQ // steps
    st, ch = SSM_TILE_ST, SSM_TILE_CH
    nh = ch // V7X_LANES
    est = (8 * steps * ch * 4 + 2 * BATCH * steps * ch * 2 + 6 * ch * 2 * st * 2 + 2 * rows * 2 * st * 4
           + rows * 2 * st * 2 + 4 * rows * ch * 4 + 8 * 8 * st * 4 + (4 << 20))
    u_spec = lambda b: pl.BlockSpec((steps, ch), lambda g, t, b=b: (b * nt + t, g))
    return pl.pallas_call(
        _ssm_prompt_body,
        out_shape=(jax.ShapeDtypeStruct((BATCH + 1, SEQ, D_MODEL), BF16),
                   jax.ShapeDtypeStruct((V7X_SUBLANES, SSM_GROUPS * SSM_STATE), F32),
                   jax.ShapeDtypeStruct((V7X_SUBLANES, SSM_GROUPS * SSM_STATE), F32)),
        grid=(SSM_TILES, nt),
        in_specs=[u_spec(0), u_spec(1), u_spec(2), u_spec(3),
                  pl.BlockSpec((1, 2 * ch, 2 * st), lambda g, t: (g, 0, 0)),
                  pl.BlockSpec((1, 2 * st, ch), lambda g, t: (g, 0, 0)),
                  pl.BlockSpec((1, 1, st), lambda g, t: (g, 0, 0)),
                  pl.BlockSpec((1, 1, st), lambda g, t: (g, 0, 0)),
                  pl.BlockSpec((1, 1, ch), lambda g, t: (g, 0, 0))],
        out_specs=(pl.BlockSpec((BATCH, steps, ch), lambda g, t: (0, t, g)),
                   pl.BlockSpec((V7X_SUBLANES, st), lambda g, t: (0, g)),
                   pl.BlockSpec((V7X_SUBLANES, st), lambda g, t: (0, g))),
        scratch_shapes=[pltpu.VMEM((nh, rows, V7X_LANES), F32), pltpu.VMEM((nh, rows, V7X_LANES), F32),
                        pltpu.VMEM((rows, 2 * st), F32), pltpu.VMEM((rows, 2 * st), BF16),
                        pltpu.VMEM((V7X_SUBLANES, 2 * st), F32),
                        pltpu.VMEM((V7X_SUBLANES, st), F32), pltpu.VMEM((V7X_SUBLANES, st), F32)],
        compiler_params=_cparams(2, est),
        name="ssm_prompt",
    )(u, u, u, u, b_mat, c_mat, are, aim, d)


def _ssm_sample_body(u_ref, h0re_ref, h0im_ref, b_ref, c_ref, are_ref, aim_ref, d_ref,
                     y_ref, hre_ref, him_ref):
    st = SSM_TILE_ST
    u = u_ref[...]
    bu = jnp.dot(u.astype(BF16), b_ref[0], preferred_element_type=F32)
    ar, ai = are_ref[0], aim_ref[0]
    h0r, h0i = h0re_ref[...], h0im_ref[...]
    hr = ar * h0r - ai * h0i + bu[:, 0:st]
    hi = ar * h0i + ai * h0r + bu[:, st:2 * st]
    hre_ref[...] = hr
    him_ref[...] = hi
    hcat = jnp.concatenate([hr, hi], axis=1).astype(BF16)
    y = jnp.dot(hcat, c_ref[0], preferred_element_type=F32) + d_ref[0] * u
    y_ref[...] = _gelu_tanh(y).astype(y_ref.dtype)


def _ssm_sample(u, h0_re, h0_im, b_mat, c_mat, are, aim, d):
    nb = DEC_BATCH
    st, ch = SSM_TILE_ST, SSM_TILE_CH
    est = 2 * (nb * ch * 6 + 4 * nb * st * 4 + 2 * ch * 2 * st * 2) + (4 << 20)
    return pl.pallas_call(
        _ssm_sample_body,
        out_shape=(jax.ShapeDtypeStruct((nb, D_MODEL), BF16),
                   jax.ShapeDtypeStruct((nb, SSM_GROUPS * SSM_STATE), F32),
                   jax.ShapeDtypeStruct((nb, SSM_GROUPS * SSM_STATE), F32)),
        grid=(SSM_TILES,),
        in_specs=[pl.BlockSpec((nb, ch), lambda g: (PROMPT_ROWS // DEC_BATCH, g)),
                  pl.BlockSpec((nb, st), lambda g: (0, g)),
                  pl.BlockSpec((nb, st), lambda g: (0, g)),
                  pl.BlockSpec((1, ch, 2 * st), lambda g: (g, 0, 0)),
                  pl.BlockSpec((1, 2 * st, ch), lambda g: (g, 0, 0)),
                  pl.BlockSpec((1, 1, st), lambda g: (g, 0, 0)),
                  pl.BlockSpec((1, 1, st), lambda g: (g, 0, 0)),
                  pl.BlockSpec((1, 1, ch), lambda g: (g, 0, 0))],
        out_specs=(pl.BlockSpec((nb, ch), lambda g: (0, g)),
                   pl.BlockSpec((nb, st), lambda g: (0, g)),
                   pl.BlockSpec((nb, st), lambda g: (0, g))),
        compiler_params=_cparams(1, est),
        name="ssm_sample",
    )(u, h0_re, h0_im, b_mat, c_mat, are, aim, d)


def kernel(x_prompt, x_sample, cache_swa_k, cache_swa_v, state_conv, state_ssm_re, state_ssm_im, ffn_norm, ffn_w_gu, ffn_w_down, mix_norm, even_w_in, even_q_norm, even_k_norm, even_sinks, even_glu_b, even_conv_w, even_conv_b, even_ln_g, even_ln_b, even_w_out, odd_w_in, odd_lam_re, odd_lam_im, odd_log_dt, odd_b_re, odd_b_im, odd_c_re, odd_c_im, odd_d, odd_w_gate, odd_w_out):
    nb = DEC_BATCH
    ssm = _ssm_params(odd_lam_re[0], odd_lam_im[0], odd_log_dt[0], odd_b_re[0], odd_b_im[0],
                      odd_c_re[0], odd_c_im[0], odd_d[0])

    x, h = _merge_rmsnorm(x_prompt.reshape(PROMPT_ROWS, D_MODEL), x_sample.reshape(nb, D_MODEL),
                          ffn_norm[0, 0])
    x = _half_ffn(x, h, ffn_w_gu, ffn_w_down, (0, 0))

    even_p = (even_sinks[0], even_q_norm[0], even_k_norm[0], even_glu_b[0], even_conv_w[0],
              even_conv_b[0], even_ln_g[0], even_ln_b[0])
    z = _matmul(_rmsnorm(x, mix_norm[0]), even_w_in, (0,), n_cols=EVEN_IN, name="even_in")
    mix, pk, pv, pc = _even_core_prompt(z, *even_p)
    mix_s, sk, sv, sc = _even_core_sample(
        z, cache_swa_k.reshape(nb, WINDOW, KV_COLS), cache_swa_v.reshape(nb, WINDOW, KV_COLS),
        state_conv.reshape(nb, CONV_WIDTH - 1, CONV_CH), *even_p)
    mix = _put_sample_rows(mix, mix_s.reshape(nb, D_MODEL))
    x = _matmul(mix, even_w_out, (0,), n_cols=D_MODEL, epilogue="res", extra=x, name="even_out")
    x = _half_ffn(x, _rmsnorm(x, ffn_norm[0, 1]), ffn_w_gu, ffn_w_down, (0, 1))

    x = _half_ffn(x, _rmsnorm(x, ffn_norm[1, 0]), ffn_w_gu, ffn_w_down, (1, 0))
    u = _matmul(_rmsnorm(x, mix_norm[1]), odd_w_in, (0,), n_cols=D_MODEL, name="odd_in")
    y3, pre8, pim8 = _ssm_prompt(u, *ssm)
    y_s, sre, sim = _ssm_sample(u, state_ssm_re.reshape(nb, -1), state_ssm_im.reshape(nb, -1), *ssm)
    y = _put_sample_rows(y3.reshape((BATCH + 1) * SEQ, D_MODEL), y_s)
    yg = _matmul(y, odd_w_gate, (0,), n_cols=D_MODEL, epilogue="gate", out_dtype=BF16, extra=y,
                 name="odd_gate")
    x = _matmul(yg, odd_w_out, (0,), n_cols=D_MODEL, epilogue="res", extra=x, name="odd_out")
    y_p, y_smp = _half_ffn(x, _rmsnorm(x, ffn_norm[1, 1]), ffn_w_gu, ffn_w_down, (1, 1), split_out=True)

    kv5 = lambda t, b: t.reshape(1, b, WINDOW, N_KV_HEADS, HEAD_DIM)
    st4 = lambda t, b: t.reshape(1, b, SSM_GROUPS, SSM_STATE)
    return (y_p.reshape(BATCH, SEQ, D_MODEL), y_smp.reshape(nb, 1, D_MODEL),
            kv5(pk, BATCH), kv5(pv, BATCH), pc.reshape(1, BATCH, CONV_WIDTH - 1, CONV_CH),
            st4(pre8[BATCH:2 * BATCH], BATCH), st4(pim8[BATCH:2 * BATCH], BATCH),
            kv5(sk, nb), kv5(sv, nb), sc.reshape(1, nb, CONV_WIDTH - 1, CONV_CH),
            st4(sre, nb), st4(sim, nb))
```

```python
import functools
import math

import jax
import jax.numpy as jnp
from jax import lax
from jax.experimental import pallas as pl
from jax.experimental.pallas import tpu as pltpu

F32 = jnp.float32
BF16 = jnp.bfloat16

D_MODEL = 4096
BATCH = 4
SEQ = 2048
DEC_BATCH = 32
PAST_LEN = 16384
HEAD_DIM = 128
N_HEADS = 16
N_KV_HEADS = 4
Q_PER_KV = 4
WINDOW = 128
ROT_DIM = 32
ROPE_THETA = 500000.0
ATT_WIDTH = 2048
CONV_CH = 2048
CONV_WIDTH = 31
Q_COLS = 2048
KV_COLS = 512
EVEN_IN = 7168
SSM_GROUPS = 256
SSM_GROUP = 16
SSM_STATE = 64
D_FF = 11008
EPS = 1e-6
NEG_INF = -1e30

V7X_VMEM_BYTES = 64 * 1024 * 1024
V7X_LANES = 128
V7X_SUBLANES = 8
VMEM_CAP = V7X_VMEM_BYTES - 3 * 1024 * 1024

PROMPT_ROWS = BATCH * SEQ
MERGE_TILE = 128
M_ROWS = PROMPT_ROWS + MERGE_TILE
ROW_TILES = 8
TM = M_ROWS // ROW_TILES
TM_DOWN = TM // 2

SSM_TILE_GROUPS = 16
SSM_TILE_CH = SSM_TILE_GROUPS * SSM_GROUP
SSM_TILE_ST = SSM_TILE_GROUPS * SSM_STATE
SSM_TILES = SSM_GROUPS // SSM_TILE_GROUPS
SSM_STEPS = 256
CONV_HALO = 32


def _cparams(n_grid, est_bytes):
    limit = int(min(VMEM_CAP, est_bytes * 5 // 4 + (4 << 20)))
    return pltpu.CompilerParams(dimension_semantics=("arbitrary",) * n_grid,
                                vmem_limit_bytes=limit)


def _sigmoid(x):
    return 0.5 * (1.0 + jnp.tanh(0.5 * x))


def _rms(x, g):
    ms = jnp.mean(x * x, axis=-1, keepdims=True)
    return x * lax.rsqrt(ms + EPS) * g


def _rmsnorm_body(x_ref, g_ref, o_ref):
    o_ref[...] = _rms(x_ref[...], g_ref[...]).astype(o_ref.dtype)


def _rmsnorm(x, g):
    d = x.shape[1]
    tr = TM // 5
    return pl.pallas_call(
        _rmsnorm_body,
        out_shape=jax.ShapeDtypeStruct((M_ROWS, d), BF16),
        grid=(M_ROWS // tr,),
        in_specs=[pl.BlockSpec((tr, d), lambda i: (i, 0)),
                  pl.BlockSpec((1, d), lambda i: (0, 0))],
        out_specs=pl.BlockSpec((tr, d), lambda i: (i, 0)),
        compiler_params=_cparams(1, 2 * tr * d * 4 + 2 * tr * d * 2),
        name="rmsnorm",
    )(x, g.reshape(1, d))


def _merge_rmsnorm_body(xp_ref, xs_ref, g_ref, x_ref, h_ref):
    i = pl.program_id(0)
    g = g_ref[...]

    @pl.when(i < PROMPT_ROWS // MERGE_TILE)
    def _prompt():
        x = xp_ref[...]
        x_ref[...] = x
        h_ref[...] = _rms(x, g).astype(h_ref.dtype)

    @pl.when(i == PROMPT_ROWS // MERGE_TILE)
    def _sample():
        xs = xs_ref[...]
        pad = MERGE_TILE - DEC_BATCH
        x_ref[0:DEC_BATCH, :] = xs
        x_ref[DEC_BATCH:MERGE_TILE, :] = jnp.zeros((pad, D_MODEL), F32)
        h_ref[0:DEC_BATCH, :] = _rms(xs, g).astype(h_ref.dtype)
        h_ref[DEC_BATCH:MERGE_TILE, :] = jnp.zeros((pad, D_MODEL), h_ref.dtype)


def _merge_rmsnorm(xp, xs, g):
    d = D_MODEL
    last = PROMPT_ROWS // MERGE_TILE - 1
    return pl.pallas_call(
        _merge_rmsnorm_body,
        out_shape=(jax.ShapeDtypeStruct((M_ROWS, d), F32), jax.ShapeDtypeStruct((M_ROWS, d), BF16)),
        grid=(M_ROWS // MERGE_TILE,),
        in_specs=[pl.BlockSpec((MERGE_TILE, d), lambda i: (jnp.minimum(i, last), 0)),
                  pl.BlockSpec((DEC_BATCH, d), lambda i: (0, 0)),
                  pl.BlockSpec((1, d), lambda i: (0, 0))],
        out_specs=(pl.BlockSpec((MERGE_TILE, d), lambda i: (i, 0)),
                   pl.BlockSpec((MERGE_TILE, d), lambda i: (i, 0))),
        compiler_params=_cparams(1, 4 * MERGE_TILE * d * 4 + 2 * MERGE_TILE * d * 2 + 2 * DEC_BATCH * d * 4),
        name="merge_rmsnorm",
    )(xp, xs, g.reshape(1, d))


def _mm_body(*refs, n_w, cast_w, epilogue, scale, side_cast, split_rows):
    a_ref = refs[0]
    w_refs = refs[1:1 + n_w]
    pos = 1 + n_w
    extra_ref = side_in_ref = side_out_ref = o2_ref = None
    if epilogue in ("res", "gate"):
        extra_ref = refs[pos]
        pos += 1
    if side_cast:
        side_in_ref = refs[pos]
        pos += 1
    o_ref = refs[pos]
    pos += 1
    if side_cast:
        side_out_ref = refs[pos]
        pos += 1
    if split_rows is not None:
        o2_ref = refs[pos]
        pos += 1
    tn = w_refs[0].shape[-1]
    if cast_w:
        wb_ref = refs[pos]

        @pl.when(pl.program_id(1) == 0)
        def _cast_weights():
            for n, w_ref in enumerate(w_refs):
                wb_ref[:, n * tn:(n + 1) * tn] = w_ref[...].astype(BF16)
    if side_cast:
        side_out_ref[...] = side_in_ref[...].astype(side_out_ref.dtype)

    a = a_ref[...].astype(BF16)
    if cast_w:
        acc = jnp.dot(a, wb_ref[...], preferred_element_type=F32)
        accs = [acc[:, n * tn:(n + 1) * tn] for n in range(n_w)]
    else:
        accs = [jnp.dot(a, w_ref[...], preferred_element_type=F32) for w_ref in w_refs]
    acc = accs[0]
    if epilogue == "plain":
        out = acc
    elif epilogue == "swiglu":
        gate, up = accs
        out = gate * _sigmoid(gate) * up
    elif epilogue == "res":
        out = extra_ref[...] + scale * acc
    elif epilogue == "gate":
        out = extra_ref[...].astype(F32) * _sigmoid(acc)
    o_ref[...] = out.astype(o_ref.dtype)

    if split_rows is not None:
        @pl.when(pl.program_id(1) == pl.num_programs(1) - 1)
        def _sample_rows():
            o2_ref[...] = out[split_rows:split_rows + DEC_BATCH, :].astype(o2_ref.dtype)


def _matmul(a, w, widx=(), *, n_cols, tm=TM, tn=512, epilogue="plain", out_dtype=F32, col_offs=(0,),
            extra=None, scale=1.0, side=None, split_out=False, name="mm"):
    k = a.shape[1]
    n_w = len(col_offs)
    cast_w = w.dtype != BF16
    nlead = len(widx)
    n_i = M_ROWS // tm
    n_j = n_cols // tn

    operands = [a]
    in_specs = [pl.BlockSpec((tm, k), lambda j, i: (i, 0))]
    for off in col_offs:
        operands.append(w)
        in_specs.append(pl.BlockSpec((None,) * nlead + (k, tn),
                                     lambda j, i, off=off: tuple(widx) + (0, j + off)))
    w_bytes = w.dtype.itemsize
    est = 2 * tm * k * a.dtype.itemsize + n_w * (2 * k * tn * w_bytes + (k * tn * 2 if cast_w else 0))
    if extra is not None:
        operands.append(extra)
        in_specs.append(pl.BlockSpec((tm, tn), lambda j, i: (i, j)))
        est += 2 * tm * tn * extra.dtype.itemsize
    out_shapes, out_specs = [], []
    if split_out:
        out_shapes.append(jax.ShapeDtypeStruct((PROMPT_ROWS, n_cols), out_dtype))
    else:
        out_shapes.append(jax.ShapeDtypeStruct((M_ROWS, n_cols), out_dtype))
    out_specs.append(pl.BlockSpec((tm, tn), lambda j, i: (i, j)))
    if side is not None:
        w_src, widx_src, rows_total = side
        slab = rows_total // (n_i * n_j)
        assert slab * n_i * n_j == rows_total and slab % 16 == 0
        cols_src = w_src.shape[-1]
        operands.append(w_src)
        in_specs.append(pl.BlockSpec((None,) * len(widx_src) + (slab, cols_src),
                                     lambda j, i: tuple(widx_src) + (j * n_i + i, 0)))
        out_shapes.append(jax.ShapeDtypeStruct((rows_total, cols_src), BF16))
        out_specs.append(pl.BlockSpec((slab, cols_src), lambda j, i: (j * n_i + i, 0)))
        est += 2 * slab * cols_src * 6
    split_rows = None
    if split_out:
        split_rows = PROMPT_ROWS - (n_i - 1) * tm
        assert split_rows % V7X_SUBLANES == 0 and 0 <= split_rows <= tm - DEC_BATCH
        out_shapes.append(jax.ShapeDtypeStruct((DEC_BATCH, n_cols), out_dtype))
        out_specs.append(pl.BlockSpec((DEC_BATCH, tn), lambda j, i: (0, j)))
    est += 2 * tm * tn * jnp.dtype(out_dtype).itemsize + (n_w + 1) * tm * tn * 4

    outs = pl.pallas_call(
        functools.partial(_mm_body, n_w=n_w, cast_w=cast_w, epilogue=epilogue, scale=scale,
                          side_cast=side is not None, split_rows=split_rows),
        out_shape=tuple(out_shapes),
        grid=(n_j, n_i),
        in_specs=in_specs,
        out_specs=tuple(out_specs),
        scratch_shapes=[pltpu.VMEM((k, n_w * tn), BF16)] if cast_w else [],
        compiler_params=_cparams(2, est),
        name=name,
    )(*operands)
    return outs[0] if len(outs) == 1 else outs


def _half_ffn(x, h, w_gu, w_down, widx, *, gu_bf16=None, next_gu=None, split_out=False):
    if gu_bf16 is None:
        gu_w, gu_idx, gu_tm = w_gu, widx, TM
    else:
        gu_w, gu_idx, gu_tm = gu_bf16, (), 2 * TM
    act, wd_bf16 = _matmul(h, gu_w, gu_idx, n_cols=D_FF, tm=gu_tm, tn=256, epilogue="swiglu",
                           out_dtype=BF16, col_offs=(0, D_FF // 256), side=(w_down, widx, D_FF),
                           name="ffn_gu")
    side = None if next_gu is None else (w_gu, next_gu, D_MODEL)
    outs = _matmul(act, wd_bf16, n_cols=D_MODEL, tm=TM_DOWN, tn=512, epilogue="res", extra=x, scale=0.5,
                   side=side, split_out=split_out, name="ffn_down")
    if next_gu is None:
        return outs, None
    return outs[0], outs[1]


def _rope_tables(pos):
    half = ROT_DIM // 2
    inv_freq = jnp.power(jnp.float32(ROPE_THETA), -jnp.arange(half, dtype=F32) * (2.0 / ROT_DIM))
    ang = pos.astype(F32)[:, None] * inv_freq[None, :]
    cos, sin = jnp.cos(ang), jnp.sin(ang)
    n = pos.shape[0]
    rest = HEAD_DIM - ROT_DIM
    c = jnp.concatenate([cos, cos, jnp.ones((n, rest), F32)], axis=1)
    sp = jnp.concatenate([jnp.zeros((n, half), F32), sin, jnp.zeros((n, rest), F32)], axis=1)
    sm = jnp.concatenate([-sin, jnp.zeros((n, HEAD_DIM - half), F32)], axis=1)
    return c, sp, sm


def _norm_rope(x, g, c, sp, sm):
    half = ROT_DIM // 2
    y = x * lax.rsqrt(jnp.mean(x * x, axis=-1, keepdims=True) + EPS) * g
    return y * c + pltpu.roll(y, half, 1) * sp + pltpu.roll(y, HEAD_DIM - half, 1) * sm


def _layer_norm_silu(c, g, b):
    mu = jnp.mean(c, axis=-1, keepdims=True)
    xc = c - mu
    var = jnp.mean(xc * xc, axis=-1, keepdims=True)
    y = xc * lax.rsqrt(var + EPS) * g + b
    return y * _sigmoid(y)


def _even_prompt_body(sinks_ref, tile_ref, *refs):
    nblk = BATCH * (SEQ // WINDOW)
    s = pl.program_id(0)

    @pl.when(s < nblk)
    def _prompt_block():
        _even_prompt_block(s % (SEQ // WINDOW), sinks_ref, *refs)

    @pl.when(s == nblk)
    def _sample_tile():
        mix_ref = refs[16]
        mix_ref[...] = tile_ref[...]


def _even_prompt_block(i, sinks_ref, q_ref, kv_ref, a0_ref, a1_ref, g0_ref, g1_ref,
                       c_ref, sp_ref, sm_ref, qn_ref, kn_ref, glub_ref, cw_ref, cb_ref,
                       lng_ref, lnb_ref,
                       mix_ref, nk_ref, nv_ref, nc_ref,
                       kprev_ref, vprev_ref, gbuf_ref, cbuf_ref):
    blk = WINDOW

    @pl.when(i == 0)
    def _reset():
        kprev_ref[...] = jnp.zeros_like(kprev_ref)
        vprev_ref[...] = jnp.zeros_like(vprev_ref)
        gbuf_ref[0:CONV_HALO, :] = jnp.zeros((CONV_HALO, CONV_CH), F32)

    c, sp, sm = c_ref[...], sp_ref[...], sm_ref[...]
    qn, kn = qn_ref[...], kn_ref[...]

    row = lax.broadcasted_iota(jnp.int32, (blk, 2 * blk), 0)
    col = lax.broadcasted_iota(jnp.int32, (blk, 2 * blk), 1)
    first_key = jnp.where(i > 0, 0, blk)
    mask = (col >= jnp.maximum(row, first_key)) & (col <= row + WINDOW)
    scale = HEAD_DIM ** -0.5
    for g in range(N_KV_HEADS):
        ks = slice(g * HEAD_DIM, (g + 1) * HEAD_DIM)
        k_cur = _norm_rope(kv_ref[:, ks], kn, c, sp, sm)
        v_cur = kv_ref[:, KV_COLS + g * HEAD_DIM:KV_COLS + (g + 1) * HEAD_DIM]
        nk_ref[0, :, ks] = k_cur
        nv_ref[0, :, ks] = v_cur
        k_cur_b = k_cur.astype(BF16)
        v_cur_b = v_cur.astype(BF16)
        k2 = jnp.concatenate([kprev_ref[:, ks], k_cur_b], axis=0)
        v2 = jnp.concatenate([vprev_ref[:, ks], v_cur_b], axis=0)
        for r in range(Q_PER_KV):
            h = g * Q_PER_KV + r
            hs = slice(h * HEAD_DIM, (h + 1) * HEAD_DIM)
            qh = _norm_rope(q_ref[:, hs], qn, c, sp, sm).astype(BF16)
            s = lax.dot_general(qh, k2, (((1,), (1,)), ((), ())),
                                preferred_element_type=F32) * scale
            s = jnp.where(mask, s, NEG_INF)
            sink = sinks_ref[h]
            m = jnp.maximum(jnp.max(s, axis=-1, keepdims=True), sink)
            p = jnp.exp(s - m)
            denom = jnp.sum(p, axis=-1, keepdims=True) + jnp.exp(sink - m)
            w = (p / denom).astype(BF16)
            o = jnp.dot(w, v2, preferred_element_type=F32)
            mix_ref[:, hs] = o.astype(mix_ref.dtype)
        kprev_ref[:, ks] = k_cur_b
        vprev_ref[:, ks] = v_cur_b

    half_ch = CONV_CH // 2
    for hh, (a_ref, g_ref) in enumerate(((a0_ref, g0_ref), (a1_ref, g1_ref))):
        cs = slice(hh * half_ch, (hh + 1) * half_ch)
        a = a_ref[...] + glub_ref[:, cs]
        gate = g_ref[...] + glub_ref[:, CONV_CH + hh * half_ch:CONV_CH + (hh + 1) * half_ch]
        gbuf_ref[CONV_HALO:CONV_HALO + blk, cs] = a * _sigmoid(gate)

    lane_chunk = V7X_LANES
    first = CONV_HALO - (CONV_WIDTH - 1)
    sub = V7X_SUBLANES

    def conv_chunk(ci, carry):
        ls = pl.ds(pl.multiple_of(ci * lane_chunk, lane_chunk), lane_chunk)
        acc = jnp.zeros((blk, lane_chunk), F32)
        for s in range(sub):
            taps = [w for w in range(CONV_WIDTH) if (first + w) % sub == s]
            rows = blk if s == 0 else blk + sub
            part = jnp.zeros((rows, lane_chunk), F32)
            for w in taps:
                base = first + w - s
                part = part + cw_ref[w:w + 1, ls] * gbuf_ref[base:base + rows, ls]
            acc = acc + part[s:s + blk, :]
        cbuf_ref[:, ls] = acc
        return carry

    lax.fori_loop(0, CONV_CH // lane_chunk, conv_chunk, 0)
    y = _layer_norm_silu(cbuf_ref[...] + cb_ref[...], lng_ref[...], lnb_ref[...])
    mix_ref[:, ATT_WIDTH:ATT_WIDTH + CONV_CH] = y.astype(mix_ref.dtype)

    @pl.when(i == SEQ // WINDOW - 1)
    def _emit_conv_state():
        nc_ref[0] = gbuf_ref[CONV_HALO + blk - (CONV_WIDTH - 1):CONV_HALO + blk, :]

    gbuf_ref[0:CONV_HALO, :] = gbuf_ref[blk:blk + CONV_HALO, :]


def _even_core_prompt(z, sample_tile, sinks, qn, kn, glub, cw, cb, lng, lnb):
    nb = SEQ // WINDOW
    blk = WINDOW
    c, sp, sm = _rope_tables(jnp.arange(SEQ, dtype=jnp.int32))
    last = BATCH * nb - 1
    rowmap = lambda col: (lambda s: (jnp.minimum(s, last), col))
    per_seq = lambda s: (jnp.minimum(s, last) // nb, 0, 0)
    tab = pl.BlockSpec((blk, HEAD_DIM), lambda s: (s % nb, 0))
    vec = lambda n: pl.BlockSpec((1, n), lambda s: (0, 0))
    est = (2 * blk * (2048 + 1024 + 4 * 1024) * 4 + 4 * blk * 4096 * 2 + 4 * blk * 512 * 4
           + (blk + CONV_HALO) * CONV_CH * 4 + 3 * blk * CONV_CH * 4 + 2 * CONV_WIDTH * CONV_CH * 4 + (8 << 20))
    return pl.pallas_call(
        _even_prompt_body,
        out_shape=(jax.ShapeDtypeStruct((M_ROWS, 2 * ATT_WIDTH), BF16),
                   jax.ShapeDtypeStruct((BATCH, blk, KV_COLS), F32),
                   jax.ShapeDtypeStruct((BATCH, blk, KV_COLS), F32),
                   jax.ShapeDtypeStruct((BATCH, CONV_WIDTH - 1, CONV_CH), F32)),
        grid=(BATCH * nb + 1,),
        in_specs=[pl.BlockSpec(memory_space=pltpu.SMEM),
                  pl.BlockSpec((MERGE_TILE, 2 * ATT_WIDTH), lambda s: (0, 0)),
                  pl.BlockSpec((blk, 2048), rowmap(0)),
                  pl.BlockSpec((blk, 1024), rowmap(2)),
                  pl.BlockSpec((blk, 1024), rowmap(3)),
                  pl.BlockSpec((blk, 1024), rowmap(4)),
                  pl.BlockSpec((blk, 1024), rowmap(5)),
                  pl.BlockSpec((blk, 1024), rowmap(6)),
                  tab, tab, tab, vec(HEAD_DIM), vec(HEAD_DIM), vec(2 * CONV_CH),
                  pl.BlockSpec((CONV_WIDTH, CONV_CH), lambda s: (0, 0)),
                  vec(CONV_CH), vec(CONV_CH), vec(CONV_CH)],
        out_specs=(pl.BlockSpec((blk, 2 * ATT_WIDTH), lambda s: (s, 0)),
                   pl.BlockSpec((1, blk, KV_COLS), per_seq),
                   pl.BlockSpec((1, blk, KV_COLS), per_seq),
                   pl.BlockSpec((1, CONV_WIDTH - 1, CONV_CH), per_seq)),
        scratch_shapes=[pltpu.VMEM((blk, KV_COLS), BF16), pltpu.VMEM((blk, KV_COLS), BF16),
                        pltpu.VMEM((blk + CONV_HALO, CONV_CH), F32), pltpu.VMEM((blk, CONV_CH), F32)],
        compiler_params=_cparams(1, est),
        name="even_core_prompt",
    )(sinks, sample_tile, z, z, z, z, z, z, c, sp, sm, qn.reshape(1, -1), kn.reshape(1, -1),
      glub.reshape(1, -1), cw, cb.reshape(1, -1), lng.reshape(1, -1), lnb.reshape(1, -1))


def _even_sample_body(sinks_ref, z_ref, ck_ref, cv_ref, sc_ref, c_ref, sp_ref, sm_ref, qn_ref, kn_ref,
                      glub_ref, cw_ref, cb_ref, lng_ref, lnb_ref,
                      mix_ref, nk_ref, nv_ref, nc_ref):
    zrow = lambda lo, hi: z_ref[0, :, lo:hi]
    c, sp, sm = c_ref[...], sp_ref[...], sm_ref[...]
    qn, kn = qn_ref[...], kn_ref[...]
    nbuf = WINDOW
    scale = HEAD_DIM ** -0.5
    qrow = lax.broadcasted_iota(jnp.int32, (V7X_SUBLANES, 1), 0)
    krow = lax.broadcasted_iota(jnp.int32, (nbuf, HEAD_DIM), 0)
    for g in range(N_KV_HEADS):
        ks = slice(g * HEAD_DIM, (g + 1) * HEAD_DIM)
        k_new = _norm_rope(zrow(Q_COLS + g * HEAD_DIM, Q_COLS + (g + 1) * HEAD_DIM), kn, c, sp, sm)
        v_new = zrow(Q_COLS + KV_COLS + g * HEAD_DIM, Q_COLS + KV_COLS + (g + 1) * HEAD_DIM)
        k_old = ck_ref[0, :, ks]
        v_old = cv_ref[0, :, ks]
        nk_ref[0, :, ks] = jnp.where(krow == nbuf - 1, k_new, pltpu.roll(k_old, nbuf - 1, 0))
        nv_ref[0, :, ks] = jnp.where(krow == nbuf - 1, v_new, pltpu.roll(v_old, nbuf - 1, 0))
        qs = [_norm_rope(zrow((g * Q_PER_KV + r) * HEAD_DIM, (g * Q_PER_KV + r + 1) * HEAD_DIM),
                         qn, c, sp, sm) for r in range(Q_PER_KV)]
        q8 = jnp.zeros((V7X_SUBLANES, HEAD_DIM), F32)
        for r in range(Q_PER_KV):
            q8 = jnp.where(qrow == r, qs[r], q8)
        q8b = q8.astype(BF16)
        s = lax.dot_general(q8b, k_old.astype(BF16), (((1,), (1,)), ((), ())),
                            preferred_element_type=F32) * scale
        k_new_r = k_new.astype(BF16).astype(F32)
        v_new_r = v_new.astype(BF16).astype(F32)
        s_self = jnp.sum(q8b.astype(F32) * k_new_r, axis=-1, keepdims=True) * scale
        sink = jnp.zeros((V7X_SUBLANES, 1), F32)
        for r in range(Q_PER_KV):
            sink = jnp.where(qrow == r, sinks_ref[g * Q_PER_KV + r], sink)
        m = jnp.maximum(jnp.maximum(jnp.max(s, axis=-1, keepdims=True), s_self), sink)
        p = jnp.exp(s - m)
        p_self = jnp.exp(s_self - m)
        denom = jnp.sum(p, axis=-1, keepdims=True) + p_self + jnp.exp(sink - m)
        o = jnp.dot((p / denom).astype(BF16), v_old.astype(BF16), preferred_element_type=F32)
        o = o + (p_self / denom).astype(BF16).astype(F32) * v_new_r
        for r in range(Q_PER_KV):
            h = g * Q_PER_KV + r
            mix_ref[0, :, h * HEAD_DIM:(h + 1) * HEAD_DIM] = o[r:r + 1, :]

    conv0 = Q_COLS + 2 * KV_COLS
    a = zrow(conv0, conv0 + CONV_CH) + glub_ref[:, 0:CONV_CH]
    gate = zrow(conv0 + CONV_CH, EVEN_IN) + glub_ref[:, CONV_CH:2 * CONV_CH]
    gl = a * _sigmoid(gate)
    nst = CONV_WIDTH - 1
    conv = jnp.sum(cw_ref[0:nst, :] * sc_ref[0], axis=0, keepdims=True) + cw_ref[nst:nst + 1, :] * gl
    y = _layer_norm_silu(conv + cb_ref[...], lng_ref[...], lnb_ref[...])
    mix_ref[0, :, ATT_WIDTH:ATT_WIDTH + CONV_CH] = y
    nc_ref[0, 0:nst - 1, :] = sc_ref[0, 1:nst, :]
    nc_ref[0, nst - 1:nst, :] = gl


def _even_core_sample(z, cache_k, cache_v, state_conv, sinks, qn, kn, glub, cw, cb, lng, lnb):
    nb = DEC_BATCH
    c, sp, sm = _rope_tables(jnp.full((1,), PAST_LEN, jnp.int32))
    vec = lambda n: pl.BlockSpec((1, n), lambda b: (0, 0))
    per_b = lambda r, n: pl.BlockSpec((1, r, n), lambda b: (b, 0, 0))
    nst = CONV_WIDTH - 1
    est = 2 * (EVEN_IN * 4 * nb + 4 * WINDOW * KV_COLS * 4 + 2 * 32 * CONV_CH * 4 + 4096 * 4 * 8) + (8 << 20)
    return pl.pallas_call(
        _even_sample_body,
        out_shape=(jax.ShapeDtypeStruct((nb, 1, 2 * ATT_WIDTH), F32),
                   jax.ShapeDtypeStruct((nb, WINDOW, KV_COLS), F32),
                   jax.ShapeDtypeStruct((nb, WINDOW, KV_COLS), F32),
                   jax.ShapeDtypeStruct((nb, nst, CONV_CH), F32)),
        grid=(nb,),
        in_specs=[pl.BlockSpec(memory_space=pltpu.SMEM),
                  per_b(1, EVEN_IN),
                  per_b(WINDOW, KV_COLS), per_b(WINDOW, KV_COLS), per_b(nst, CONV_CH),
                  vec(HEAD_DIM), vec(HEAD_DIM), vec(HEAD_DIM), vec(HEAD_DIM), vec(HEAD_DIM),
                  vec(2 * CONV_CH), pl.BlockSpec((CONV_WIDTH, CONV_CH), lambda b: (0, 0)),
                  vec(CONV_CH), vec(CONV_CH), vec(CONV_CH)],
        out_specs=(per_b(1, 2 * ATT_WIDTH), per_b(WINDOW, KV_COLS), per_b(WINDOW, KV_COLS),
                   per_b(nst, CONV_CH)),
        compiler_params=_cparams(1, est),
        name="even_core_sample",
    )(sinks, z[PROMPT_ROWS:PROMPT_ROWS + nb].reshape(nb, 1, EVEN_IN), cache_k, cache_v, state_conv, c, sp, sm,
      qn.reshape(1, -1), kn.reshape(1, -1), glub.reshape(1, -1), cw, cb.reshape(1, -1),
      lng.reshape(1, -1), lnb.reshape(1, -1))


def _sample_tile(sample_rows):
    pad = jnp.zeros((MERGE_TILE - DEC_BATCH, sample_rows.shape[1]), BF16)
    return jnp.concatenate([sample_rows.astype(BF16), pad], axis=0)


def _ssm_prep_body(lre_ref, lim_ref, ldt_ref, btre_ref, btim_ref,
                   are_ref, aim_ref, bbre_ref, bbim_ref, abre_ref, abim_ref, cre_ref, cim_ref):
    lre, lim = lre_ref[...], lim_ref[...]
    dt = jnp.exp(ldt_ref[...])
    mag = jnp.exp(lre * dt)
    ang = lim * dt
    are = mag * jnp.cos(ang)
    aim = mag * jnp.sin(ang)
    are_ref[...] = are
    aim_ref[...] = aim
    nre, nim = are - 1.0, aim
    den = lre * lre + lim * lim
    cre_ref[...] = (nre * lre + nim * lim) / den
    cim_ref[...] = (nim * lre - nre * lim) / den

    def per_group(g, carry):
        cr = cre_ref[pl.ds(g, 1), :]
        ci = cim_ref[pl.ds(g, 1), :]
        br, bi = btre_ref[g], btim_ref[g]
        bbr = cr * br - ci * bi
        bbi = cr * bi + ci * br
        bbre_ref[g] = bbr
        bbim_ref[g] = bbi
        ar = are_ref[pl.ds(g, 1), :]
        ai = aim_ref[pl.ds(g, 1), :]
        abre_ref[g] = ar * bbr - ai * bbi
        abim_ref[g] = ar * bbi + ai * bbr
        return carry

    lax.fori_loop(0, SSM_GROUPS, per_group, 0)


def _ssm_layout_body(bbre_ref, bbim_ref, abre_ref, abim_ref, cre_ref, cim_ref, b_ref, c_ref):
    n, ch, st = SSM_STATE, SSM_TILE_CH, SSM_TILE_ST
    p_shift = SSM_GROUP.bit_length() - 1
    n_shift = SSM_STATE.bit_length() - 1
    iota = lambda shape, d: lax.broadcasted_iota(jnp.int32, shape, d)
    spread = jnp.where((iota((n, st), 1) & (n - 1)) == iota((n, st), 0), 1.0, 0.0).astype(BF16)
    own_b = (iota((ch, st), 0) >> p_shift) == (iota((ch, st), 1) >> n_shift)
    for k, src_ref in enumerate((bbre_ref, bbim_ref, abre_ref, abim_ref)):
        rep = jnp.dot(src_ref[...].astype(BF16), spread, preferred_element_type=F32)
        r0, c0 = (k // 2) * ch, (k % 2) * st
        b_ref[0, r0:r0 + ch, c0:c0 + st] = jnp.where(own_b, rep, 0.0).astype(BF16)
    spread_t = jnp.where((iota((st, n), 0) & (n - 1)) == iota((st, n), 1), 1.0, 0.0).astype(BF16)
    own_c = (iota((st, ch), 0) >> n_shift) == (iota((st, ch), 1) >> p_shift)
    for k, (src_ref, sign) in enumerate(((cre_ref, 1.0), (cim_ref, -1.0))):
        rep = lax.dot_general(spread_t, src_ref[...].astype(BF16), (((1,), (1,)), ((), ())),
                              preferred_element_type=F32)
        c_ref[0, k * st:(k + 1) * st, :] = jnp.where(own_c, sign * rep, 0.0).astype(BF16)


def _ssm_params(lam_re, lam_im, log_dt, b_re, b_im, c_re, c_im, d_skip):
    g, n, p = SSM_GROUPS, SSM_STATE, SSM_GROUP
    bt_re = jnp.swapaxes(b_re, 1, 2)
    bt_im = jnp.swapaxes(b_im, 1, 2)
    gn = jax.ShapeDtypeStruct((g, n), F32)
    gpn = jax.ShapeDtypeStruct((g, p, n), F32)
    are, aim, bbre, bbim, abre, abim = pl.pallas_call(
        _ssm_prep_body,
        out_shape=(gn, gn, gpn, gpn, gpn, gpn),
        scratch_shapes=[pltpu.VMEM((g, n), F32), pltpu.VMEM((g, n), F32)],
        name="ssm_prep",
    )(lam_re, lam_im, log_dt.reshape(g, 1), bt_re, bt_im)

    t, tg = SSM_TILES, SSM_TILE_GROUPS
    ch, st = SSM_TILE_CH, SSM_TILE_ST
    rows_gp = lambda m: m.reshape(g * p, n)
    src_spec = pl.BlockSpec((ch, n), lambda i: (i, 0))
    b_mat, c_mat = pl.pallas_call(
        _ssm_layout_body,
        out_shape=(jax.ShapeDtypeStruct((t, 2 * ch, 2 * st), BF16),
                   jax.ShapeDtypeStruct((t, 2 * st, ch), BF16)),
        grid=(t,),
        in_specs=[src_spec] * 6,
        out_specs=(pl.BlockSpec((1, 2 * ch, 2 * st), lambda i: (i, 0, 0)),
                   pl.BlockSpec((1, 2 * st, ch), lambda i: (i, 0, 0))),
        compiler_params=_cparams(1, 2 * (2 * ch * 2 * st * 2 + 2 * st * ch * 2) + 8 * ch * st * 4),
        name="ssm_layout",
    )(rows_gp(bbre), rows_gp(bbim), rows_gp(abre), rows_gp(abim), rows_gp(c_re), rows_gp(c_im))
    return (b_mat, c_mat, are.reshape(t, 1, tg * n), aim.reshape(t, 1, tg * n),
            d_skip.reshape(t, 1, tg * p))


def _gelu_tanh(y):
    return 0.5 * y * (1.0 + jnp.tanh(math.sqrt(2.0 / math.pi) * (y + 0.044715 * (y * y * y))))


def _ssm_prompt_body(u0_ref, u1_ref, u2_ref, u3_ref, tile_ref, b_ref, c_ref, are_ref, aim_ref, d_ref,
                     y_ref, hre_ref, him_ref,
                     us_ref, ys_ref, bu_ref, h_ref, carry_ref, p_re_ref, p_im_ref):
    st = SSM_TILE_ST
    sub = V7X_SUBLANES
    nl = st // V7X_LANES
    nh = SSM_TILE_CH // V7X_LANES
    tt = pl.program_id(1)
    steps = u0_ref.shape[0]
    rows = steps * BATCH

    @pl.when(tt == 0)
    def _init():
        carry_ref[...] = jnp.zeros_like(carry_ref)
        ar = jnp.broadcast_to(are_ref[0], (sub, st))
        ai = jnp.broadcast_to(aim_ref[0], (sub, st))
        lo = lax.broadcasted_iota(jnp.int32, (sub, st), 0) < BATCH
        p_re_ref[...] = jnp.where(lo, ar, ar * ar - ai * ai)
        p_im_ref[...] = jnp.where(lo, ai, 2.0 * ar * ai)

    for b, ub_ref in enumerate((u0_ref, u1_ref, u2_ref, u3_ref)):
        for hh in range(nh):
            us_ref[hh, pl.ds(b, steps, stride=BATCH), :] = ub_ref[:, hh * V7X_LANES:(hh + 1) * V7X_LANES]
    u = jnp.concatenate([us_ref[hh] for hh in range(nh)], axis=1)
    second = (lax.broadcasted_iota(jnp.int32, u.shape, 0) % sub) >= BATCH
    u_prev = jnp.where(second, pltpu.roll(u, BATCH, 0), 0.0)
    u2 = jnp.concatenate([u, u_prev], axis=1).astype(BF16)
    bu_ref[...] = jnp.dot(u2, b_ref[0], preferred_element_type=F32)

    lo8 = lax.broadcasted_iota(jnp.int32, (sub, V7X_LANES), 0) < BATCH

    pack = 2 * sub

    def step(i, carry):
        r0 = pl.multiple_of(i * pack, pack)
        new = []
        for j in range(nl):
            lre = slice(j * V7X_LANES, (j + 1) * V7X_LANES)
            lim = slice(st + j * V7X_LANES, st + (j + 1) * V7X_LANES)
            pr, pi = p_re_ref[:, lre], p_im_ref[:, lre]
            hr, hi = carry[2 * j], carry[2 * j + 1]
            outs_r, outs_i = [], []
            for k in range(2):
                yr = bu_ref[pl.ds(r0 + k * sub, sub), lre]
                yi = bu_ref[pl.ds(r0 + k * sub, sub), lim]
                hbr = jnp.where(lo8, pltpu.roll(hr, BATCH, 0), hr)
                hbi = jnp.where(lo8, pltpu.roll(hi, BATCH, 0), hi)
                hr = yr + pr * hbr - pi * hbi
                hi = yi + pr * hbi + pi * hbr
                outs_r.append(hr)
                outs_i.append(hi)
            h_ref[pl.ds(r0, pack), lre] = jnp.concatenate(outs_r, axis=0).astype(h_ref.dtype)
            h_ref[pl.ds(r0, pack), lim] = jnp.concatenate(outs_i, axis=0).astype(h_ref.dtype)
            new += [hr, hi]
        return tuple(new)

    init = []
    for j in range(nl):
        init += [carry_ref[:, j * V7X_LANES:(j + 1) * V7X_LANES],
                 carry_ref[:, st + j * V7X_LANES:st + (j + 1) * V7X_LANES]]
    final = lax.fori_loop(0, rows // pack, step, tuple(init))
    for j in range(nl):
        carry_ref[:, j * V7X_LANES:(j + 1) * V7X_LANES] = final[2 * j]
        carry_ref[:, st + j * V7X_LANES:st + (j + 1) * V7X_LANES] = final[2 * j + 1]

    y = jnp.dot(h_ref[...], c_ref[0], preferred_element_type=F32) + d_ref[0] * u
    y = _gelu_tanh(y)
    for hh in range(nh):
        ys_ref[hh] = y[:, hh * V7X_LANES:(hh + 1) * V7X_LANES]
    for b in range(BATCH):
        for hh in range(nh):
            y_ref[b, :, hh * V7X_LANES:(hh + 1) * V7X_LANES] = (
                ys_ref[hh, pl.ds(b, steps, stride=BATCH), :].astype(y_ref.dtype))
    y_ref[BATCH] = jnp.zeros((steps, SSM_TILE_CH), y_ref.dtype)

    @pl.when(tt == 0)
    def _sample_tile():
        y_ref[BATCH, 0:MERGE_TILE, :] = tile_ref[...]

    hre_ref[...] = carry_ref[:, 0:st]
    him_ref[...] = carry_ref[:, st:2 * st]


def _ssm_prompt(u, sample_tile, b_mat, c_mat, are, aim, d):
    steps = SSM_STEPS
    rows = steps * BATCH
    nt = SEQ // steps
    st, ch = SSM_TILE_ST, SSM_TILE_CH
    nh = ch // V7X_LANES
    est = (8 * steps * ch * 4 + 2 * BATCH * steps * ch * 2 + 6 * ch * 2 * st * 2 + 2 * rows * 2 * st * 4
           + rows * 2 * st * 2 + 4 * rows * ch * 4 + 8 * 8 * st * 4 + (4 << 20))
    u_spec = lambda b: pl.BlockSpec((steps, ch), lambda g, t, b=b: (b * nt + t, g))
    return pl.pallas_call(
        _ssm_prompt_body,
        out_shape=(jax.ShapeDtypeStruct((BATCH + 1, SEQ, D_MODEL), BF16),
                   jax.ShapeDtypeStruct((V7X_SUBLANES, SSM_GROUPS * SSM_STATE), F32),
                   jax.ShapeDtypeStruct((V7X_SUBLANES, SSM_GROUPS * SSM_STATE), F32)),
        grid=(SSM_TILES, nt),
        in_specs=[u_spec(0), u_spec(1), u_spec(2), u_spec(3),
                  pl.BlockSpec((MERGE_TILE, ch), lambda g, t: (0, g)),
                  pl.BlockSpec((1, 2 * ch, 2 * st), lambda g, t: (g, 0, 0)),
                  pl.BlockSpec((1, 2 * st, ch), lambda g, t: (g, 0, 0)),
                  pl.BlockSpec((1, 1, st), lambda g, t: (g, 0, 0)),
                  pl.BlockSpec((1, 1, st), lambda g, t: (g, 0, 0)),
                  pl.BlockSpec((1, 1, ch), lambda g, t: (g, 0, 0))],
        out_specs=(pl.BlockSpec((BATCH + 1, steps, ch), lambda g, t: (0, t, g)),
                   pl.BlockSpec((V7X_SUBLANES, st), lambda g, t: (0, g)),
                   pl.BlockSpec((V7X_SUBLANES, st), lambda g, t: (0, g))),
        scratch_shapes=[pltpu.VMEM((nh, rows, V7X_LANES), F32), pltpu.VMEM((nh, rows, V7X_LANES), F32),
                        pltpu.VMEM((rows, 2 * st), F32), pltpu.VMEM((rows, 2 * st), BF16),
                        pltpu.VMEM((V7X_SUBLANES, 2 * st), F32),
                        pltpu.VMEM((V7X_SUBLANES, st), F32), pltpu.VMEM((V7X_SUBLANES, st), F32)],
        compiler_params=_cparams(2, est),
        name="ssm_prompt",
    )(u, u, u, u, sample_tile, b_mat, c_mat, are, aim, d)


def _ssm_sample_body(u_ref, h0re_ref, h0im_ref, b_ref, c_ref, are_ref, aim_ref, d_ref,
                     y_ref, hre_ref, him_ref):
    st = SSM_TILE_ST
    u = u_ref[...]
    bu = jnp.dot(u.astype(BF16), b_ref[0], preferred_element_type=F32)
    ar, ai = are_ref[0], aim_ref[0]
    h0r, h0i = h0re_ref[...], h0im_ref[...]
    hr = ar * h0r - ai * h0i + bu[:, 0:st]
    hi = ar * h0i + ai * h0r + bu[:, st:2 * st]
    hre_ref[...] = hr
    him_ref[...] = hi
    hcat = jnp.concatenate([hr, hi], axis=1).astype(BF16)
    y = jnp.dot(hcat, c_ref[0], preferred_element_type=F32) + d_ref[0] * u
    y_ref[...] = _gelu_tanh(y).astype(y_ref.dtype)


def _ssm_sample(u, h0_re, h0_im, b_mat, c_mat, are, aim, d):
    nb = DEC_BATCH
    st, ch = SSM_TILE_ST, SSM_TILE_CH
    est = 2 * (nb * ch * 6 + 4 * nb * st * 4 + 2 * ch * 2 * st * 2) + (4 << 20)
    return pl.pallas_call(
        _ssm_sample_body,
        out_shape=(jax.ShapeDtypeStruct((nb, D_MODEL), BF16),
                   jax.ShapeDtypeStruct((nb, SSM_GROUPS * SSM_STATE), F32),
                   jax.ShapeDtypeStruct((nb, SSM_GROUPS * SSM_STATE), F32)),
        grid=(SSM_TILES,),
        in_specs=[pl.BlockSpec((nb, ch), lambda g: (PROMPT_ROWS // DEC_BATCH, g)),
                  pl.BlockSpec((nb, st), lambda g: (0, g)),
                  pl.BlockSpec((nb, st), lambda g: (0, g)),
                  pl.BlockSpec((1, ch, 2 * st), lambda g: (g, 0, 0)),
                  pl.BlockSpec((1, 2 * st, ch), lambda g: (g, 0, 0)),
                  pl.BlockSpec((1, 1, st), lambda g: (g, 0, 0)),
                  pl.BlockSpec((1, 1, st), lambda g: (g, 0, 0)),
                  pl.BlockSpec((1, 1, ch), lambda g: (g, 0, 0))],
        out_specs=(pl.BlockSpec((nb, ch), lambda g: (0, g)),
                   pl.BlockSpec((nb, st), lambda g: (0, g)),
                   pl.BlockSpec((nb, st), lambda g: (0, g))),
        compiler_params=_cparams(1, est),
        name="ssm_sample",
    )(u, h0_re, h0_im, b_mat, c_mat, are, aim, d)


def kernel(x_prompt, x_sample, cache_swa_k, cache_swa_v, state_conv, state_ssm_re, state_ssm_im, ffn_norm, ffn_w_gu, ffn_w_down, mix_norm, even_w_in, even_q_norm, even_k_norm, even_sinks, even_glu_b, even_conv_w, even_conv_b, even_ln_g, even_ln_b, even_w_out, odd_w_in, odd_lam_re, odd_lam_im, odd_log_dt, odd_b_re, odd_b_im, odd_c_re, odd_c_im, odd_d, odd_w_gate, odd_w_out):
    nb = DEC_BATCH
    ssm = _ssm_params(odd_lam_re[0], odd_lam_im[0], odd_log_dt[0], odd_b_re[0], odd_b_im[0],
                      odd_c_re[0], odd_c_im[0], odd_d[0])

    x, h = _merge_rmsnorm(x_prompt.reshape(PROMPT_ROWS, D_MODEL), x_sample.reshape(nb, D_MODEL),
                          ffn_norm[0, 0])
    x, gu_next = _half_ffn(x, h, ffn_w_gu, ffn_w_down, (0, 0), next_gu=(0, 1))

    even_p = (even_sinks[0], even_q_norm[0], even_k_norm[0], even_glu_b[0], even_conv_w[0],
              even_conv_b[0], even_ln_g[0], even_ln_b[0])
    z = _matmul(_rmsnorm(x, mix_norm[0]), even_w_in, (0,), n_cols=EVEN_IN, name="even_in")
    mix_s, sk, sv, sc = _even_core_sample(
        z, cache_swa_k.reshape(nb, WINDOW, KV_COLS), cache_swa_v.reshape(nb, WINDOW, KV_COLS),
        state_conv.reshape(nb, CONV_WIDTH - 1, CONV_CH), *even_p)
    mix, pk, pv, pc = _even_core_prompt(z, _sample_tile(mix_s.reshape(nb, D_MODEL)), *even_p)
    x = _matmul(mix, even_w_out, (0,), n_cols=D_MODEL, epilogue="res", extra=x, name="even_out")
    x, gu_next = _half_ffn(x, _rmsnorm(x, ffn_norm[0, 1]), ffn_w_gu, ffn_w_down, (0, 1),
                           gu_bf16=gu_next, next_gu=(1, 0))

    x, gu_next = _half_ffn(x, _rmsnorm(x, ffn_norm[1, 0]), ffn_w_gu, ffn_w_down, (1, 0),
                           gu_bf16=gu_next, next_gu=(1, 1))
    u = _matmul(_rmsnorm(x, mix_norm[1]), odd_w_in, (0,), n_cols=D_MODEL, name="odd_in")
    y_s, sre, sim = _ssm_sample(u, state_ssm_re.reshape(nb, -1), state_ssm_im.reshape(nb, -1), *ssm)
    y3, pre8, pim8 = _ssm_prompt(u, _sample_tile(y_s), *ssm)
    y = y3.reshape((BATCH + 1) * SEQ, D_MODEL)
    yg = _matmul(y, odd_w_gate, (0,), n_cols=D_MODEL, epilogue="gate", out_dtype=BF16, extra=y,
                 name="odd_gate")
    x = _matmul(yg, odd_w_out, (0,), n_cols=D_MODEL, epilogue="res", extra=x, name="odd_out")
    (y_p, y_smp), _ = _half_ffn(x, _rmsnorm(x, ffn_norm[1, 1]), ffn_w_gu, ffn_w_down, (1, 1),
                                gu_bf16=gu_next, split_out=True)

    kv5 = lambda t, b: t.reshape(1, b, WINDOW, N_KV_HEADS, HEAD_DIM)
    st4 = lambda t, b: t.reshape(1, b, SSM_GROUPS, SSM_STATE)
    return (y_p.reshape(BATCH, SEQ, D_MODEL), y_smp.reshape(nb, 1, D_MODEL),
            kv5(pk, BATCH), kv5(pv, BATCH), pc.reshape(1, BATCH, CONV_WIDTH - 1, CONV_CH),
            st4(pre8[BATCH:2 * BATCH], BATCH), st4(pim8[BATCH:2 * BATCH], BATCH),
            kv5(sk, nb), kv5(sv, nb), sc.reshape(1, nb, CONV_WIDTH - 1, CONV_CH),
            st4(sre, nb), st4(sim, nb))
```

```python
import functools
import math

import jax
import jax.numpy as jnp
from jax import lax
from jax.experimental import pallas as pl
from jax.experimental.pallas import tpu as pltpu

F32 = jnp.float32
BF16 = jnp.bfloat16

D_MODEL = 4096
BATCH = 4
SEQ = 2048
DEC_BATCH = 32
PAST_LEN = 16384
HEAD_DIM = 128
N_HEADS = 16
N_KV_HEADS = 4
Q_PER_KV = 4
WINDOW = 128
ROT_DIM = 32
ROPE_THETA = 500000.0
ATT_WIDTH = 2048
CONV_CH = 2048
CONV_WIDTH = 31
Q_COLS = 2048
KV_COLS = 512
EVEN_IN = 7168
SSM_GROUPS = 256
SSM_GROUP = 16
SSM_STATE = 64
D_FF = 11008
EPS = 1e-6
NEG_INF = -1e30

V7X_VMEM_BYTES = 64 * 1024 * 1024
V7X_LANES = 128
V7X_SUBLANES = 8
VMEM_CAP = V7X_VMEM_BYTES - 3 * 1024 * 1024

PROMPT_ROWS = BATCH * SEQ
MERGE_TILE = 128
M_ROWS = PROMPT_ROWS + MERGE_TILE
ROW_TILES = 8
TM = M_ROWS // ROW_TILES
TM_DOWN = TM // 2

SSM_TILE_GROUPS = 16
SSM_TILE_CH = SSM_TILE_GROUPS * SSM_GROUP
SSM_TILE_ST = SSM_TILE_GROUPS * SSM_STATE
SSM_TILES = SSM_GROUPS // SSM_TILE_GROUPS
SSM_STEPS = 256
SSM_PARTS = 4
CONV_HALO = 32


def _cparams(n_grid, est_bytes):
    limit = int(min(VMEM_CAP, est_bytes * 5 // 4 + (4 << 20)))
    return pltpu.CompilerParams(dimension_semantics=("arbitrary",) * n_grid,
                                vmem_limit_bytes=limit)


def _sigmoid(x):
    return 0.5 * (1.0 + jnp.tanh(0.5 * x))


def _rms(x, g):
    ms = jnp.mean(x * x, axis=-1, keepdims=True)
    return x * lax.rsqrt(ms + EPS) * g


def _rmsnorm_body(x_ref, g_ref, o_ref):
    o_ref[...] = _rms(x_ref[...], g_ref[...]).astype(o_ref.dtype)


def _rmsnorm(x, g):
    d = x.shape[1]
    tr = 2 * TM // 5
    return pl.pallas_call(
        _rmsnorm_body,
        out_shape=jax.ShapeDtypeStruct((M_ROWS, d), BF16),
        grid=(M_ROWS // tr,),
        in_specs=[pl.BlockSpec((tr, d), lambda i: (i, 0)),
                  pl.BlockSpec((1, d), lambda i: (0, 0))],
        out_specs=pl.BlockSpec((tr, d), lambda i: (i, 0)),
        compiler_params=_cparams(1, 2 * tr * d * 4 + 2 * tr * d * 2),
        name="rmsnorm",
    )(x, g.reshape(1, d))


def _merge_rmsnorm_body(xp_ref, xs_ref, g_ref, x_ref, h_ref):
    i = pl.program_id(0)
    g = g_ref[...]

    @pl.when(i < PROMPT_ROWS // MERGE_TILE)
    def _prompt():
        x = xp_ref[...]
        x_ref[...] = x
        h_ref[...] = _rms(x, g).astype(h_ref.dtype)

    @pl.when(i == PROMPT_ROWS // MERGE_TILE)
    def _sample():
        xs = xs_ref[...]
        pad = MERGE_TILE - DEC_BATCH
        x_ref[0:DEC_BATCH, :] = xs
        x_ref[DEC_BATCH:MERGE_TILE, :] = jnp.zeros((pad, D_MODEL), F32)
        h_ref[0:DEC_BATCH, :] = _rms(xs, g).astype(h_ref.dtype)
        h_ref[DEC_BATCH:MERGE_TILE, :] = jnp.zeros((pad, D_MODEL), h_ref.dtype)


def _merge_rmsnorm(xp, xs, g):
    d = D_MODEL
    last = PROMPT_ROWS // MERGE_TILE - 1
    return pl.pallas_call(
        _merge_rmsnorm_body,
        out_shape=(jax.ShapeDtypeStruct((M_ROWS, d), F32), jax.ShapeDtypeStruct((M_ROWS, d), BF16)),
        grid=(M_ROWS // MERGE_TILE,),
        in_specs=[pl.BlockSpec((MERGE_TILE, d), lambda i: (jnp.minimum(i, last), 0)),
                  pl.BlockSpec((DEC_BATCH, d), lambda i: (0, 0)),
                  pl.BlockSpec((1, d), lambda i: (0, 0))],
        out_specs=(pl.BlockSpec((MERGE_TILE, d), lambda i: (i, 0)),
                   pl.BlockSpec((MERGE_TILE, d), lambda i: (i, 0))),
        compiler_params=_cparams(1, 4 * MERGE_TILE * d * 4 + 2 * MERGE_TILE * d * 2 + 2 * DEC_BATCH * d * 4),
        name="merge_rmsnorm",
    )(xp, xs, g.reshape(1, d))


def _mm_body(*refs, n_w, cast_w, epilogue, scale, side_cast, split_rows):
    a_ref = refs[0]
    w_refs = refs[1:1 + n_w]
    pos = 1 + n_w
    extra_ref = side_in_ref = side_out_ref = o2_ref = None
    if epilogue in ("res", "gate"):
        extra_ref = refs[pos]
        pos += 1
    if side_cast:
        side_in_ref = refs[pos]
        pos += 1
    o_ref = refs[pos]
    pos += 1
    if side_cast:
        side_out_ref = refs[pos]
        pos += 1
    if split_rows is not None:
        o2_ref = refs[pos]
        pos += 1
    tn = w_refs[0].shape[-1]
    if cast_w:
        wb_ref = refs[pos]

        @pl.when(pl.program_id(1) == 0)
        def _cast_weights():
            for n, w_ref in enumerate(w_refs):
                wb_ref[:, n * tn:(n + 1) * tn] = w_ref[...].astype(BF16)
    if side_cast:
        side_out_ref[...] = side_in_ref[...].astype(side_out_ref.dtype)

    a = a_ref[...].astype(BF16)
    if cast_w:
        acc = jnp.dot(a, wb_ref[...], preferred_element_type=F32)
        accs = [acc[:, n * tn:(n + 1) * tn] for n in range(n_w)]
    else:
        accs = [jnp.dot(a, w_ref[...], preferred_element_type=F32) for w_ref in w_refs]
    acc = accs[0]
    if epilogue == "plain":
        out = acc
    elif epilogue == "swiglu":
        gate, up = accs
        out = gate * _sigmoid(gate) * up
    elif epilogue == "res":
        out = extra_ref[...] + scale * acc
    elif epilogue == "gate":
        out = extra_ref[...].astype(F32) * _sigmoid(acc)
    o_ref[...] = out.astype(o_ref.dtype)

    if split_rows is not None:
        @pl.when(pl.program_id(1) == pl.num_programs(1) - 1)
        def _sample_rows():
            o2_ref[...] = out[split_rows:split_rows + DEC_BATCH, :].astype(o2_ref.dtype)


def _matmul(a, w, widx=(), *, n_cols, tm=TM, tn=512, epilogue="plain", out_dtype=F32, col_offs=(0,),
            extra=None, scale=1.0, side=None, split_out=False, name="mm"):
    k = a.shape[1]
    n_w = len(col_offs)
    cast_w = w.dtype != BF16
    nlead = len(widx)
    n_i = M_ROWS // tm
    n_j = n_cols // tn

    operands = [a]
    in_specs = [pl.BlockSpec((tm, k), lambda j, i: (i, 0))]
    for off in col_offs:
        operands.append(w)
        in_specs.append(pl.BlockSpec((None,) * nlead + (k, tn),
                                     lambda j, i, off=off: tuple(widx) + (0, j + off)))
    w_bytes = w.dtype.itemsize
    est = 2 * tm * k * a.dtype.itemsize + n_w * (2 * k * tn * w_bytes + (k * tn * 2 if cast_w else 0))
    if extra is not None:
        operands.append(extra)
        in_specs.append(pl.BlockSpec((tm, tn), lambda j, i: (i, j)))
        est += 2 * tm * tn * extra.dtype.itemsize
    out_shapes, out_specs = [], []
    if split_out:
        out_shapes.append(jax.ShapeDtypeStruct((PROMPT_ROWS, n_cols), out_dtype))
    else:
        out_shapes.append(jax.ShapeDtypeStruct((M_ROWS, n_cols), out_dtype))
    out_specs.append(pl.BlockSpec((tm, tn), lambda j, i: (i, j)))
    if side is not None:
        w_src, widx_src, rows_total = side
        slab = rows_total // (n_i * n_j)
        assert slab * n_i * n_j == rows_total and slab % 16 == 0
        cols_src = w_src.shape[-1]
        operands.append(w_src)
        in_specs.append(pl.BlockSpec((None,) * len(widx_src) + (slab, cols_src),
                                     lambda j, i: tuple(widx_src) + (j * n_i + i, 0)))
        out_shapes.append(jax.ShapeDtypeStruct((rows_total, cols_src), BF16))
        out_specs.append(pl.BlockSpec((slab, cols_src), lambda j, i: (j * n_i + i, 0)))
        est += 2 * slab * cols_src * 6
    split_rows = None
    if split_out:
        split_rows = PROMPT_ROWS - (n_i - 1) * tm
        assert split_rows % V7X_SUBLANES == 0 and 0 <= split_rows <= tm - DEC_BATCH
        out_shapes.append(jax.ShapeDtypeStruct((DEC_BATCH, n_cols), out_dtype))
        out_specs.append(pl.BlockSpec((DEC_BATCH, tn), lambda j, i: (0, j)))
    est += 2 * tm * tn * jnp.dtype(out_dtype).itemsize + (n_w + 1) * tm * tn * 4

    outs = pl.pallas_call(
        functools.partial(_mm_body, n_w=n_w, cast_w=cast_w, epilogue=epilogue, scale=scale,
                          side_cast=side is not None, split_rows=split_rows),
        out_shape=tuple(out_shapes),
        grid=(n_j, n_i),
        in_specs=in_specs,
        out_specs=tuple(out_specs),
        scratch_shapes=[pltpu.VMEM((k, n_w * tn), BF16)] if cast_w else [],
        compiler_params=_cparams(2, est),
        name=name,
    )(*operands)
    return outs[0] if len(outs) == 1 else outs


def _half_ffn(x, h, w_gu, w_down, widx, *, gu_bf16=None, next_gu=None, split_out=False):
    if gu_bf16 is None:
        gu_w, gu_idx, gu_tm = w_gu, widx, TM
    else:
        gu_w, gu_idx, gu_tm = gu_bf16, (), 2 * TM
    act, wd_bf16 = _matmul(h, gu_w, gu_idx, n_cols=D_FF, tm=gu_tm, tn=256, epilogue="swiglu",
                           out_dtype=BF16, col_offs=(0, D_FF // 256), side=(w_down, widx, D_FF),
                           name="ffn_gu")
    side = None if next_gu is None else (w_gu, next_gu, D_MODEL)
    outs = _matmul(act, wd_bf16, n_cols=D_MODEL, tm=TM_DOWN, tn=512, epilogue="res", extra=x, scale=0.5,
                   side=side, split_out=split_out, name="ffn_down")
    if next_gu is None:
        return outs, None
    return outs[0], outs[1]


def _rope_tables(pos):
    half = ROT_DIM // 2
    inv_freq = jnp.power(jnp.float32(ROPE_THETA), -jnp.arange(half, dtype=F32) * (2.0 / ROT_DIM))
    ang = pos.astype(F32)[:, None] * inv_freq[None, :]
    cos, sin = jnp.cos(ang), jnp.sin(ang)
    n = pos.shape[0]
    rest = HEAD_DIM - ROT_DIM
    c = jnp.concatenate([cos, cos, jnp.ones((n, rest), F32)], axis=1)
    sp = jnp.concatenate([jnp.zeros((n, half), F32), sin, jnp.zeros((n, rest), F32)], axis=1)
    sm = jnp.concatenate([-sin, jnp.zeros((n, HEAD_DIM - half), F32)], axis=1)
    return c, sp, sm


def _norm_rope(x, g, c, sp, sm):
    half = ROT_DIM // 2
    y = x * lax.rsqrt(jnp.mean(x * x, axis=-1, keepdims=True) + EPS) * g
    return y * c + pltpu.roll(y, half, 1) * sp + pltpu.roll(y, HEAD_DIM - half, 1) * sm


def _layer_norm_silu(c, g, b):
    mu = jnp.mean(c, axis=-1, keepdims=True)
    xc = c - mu
    var = jnp.mean(xc * xc, axis=-1, keepdims=True)
    y = xc * lax.rsqrt(var + EPS) * g + b
    return y * _sigmoid(y)


def _even_prompt_body(sinks_ref, tile_ref, *refs):
    nblk = BATCH * (SEQ // WINDOW)
    s = pl.program_id(0)

    @pl.when(s < nblk)
    def _prompt_block():
        _even_prompt_block(s % (SEQ // WINDOW), sinks_ref, *refs)

    @pl.when(s == nblk)
    def _sample_tile():
        mix_ref = refs[16]
        mix_ref[...] = tile_ref[...]


def _even_prompt_block(i, sinks_ref, q_ref, kv_ref, a0_ref, a1_ref, g0_ref, g1_ref,
                       c_ref, sp_ref, sm_ref, qn_ref, kn_ref, glub_ref, cw_ref, cb_ref,
                       lng_ref, lnb_ref,
                       mix_ref, nk_ref, nv_ref, nc_ref,
                       kprev_ref, vprev_ref, gbuf_ref, cbuf_ref):
    blk = WINDOW

    @pl.when(i == 0)
    def _reset():
        kprev_ref[...] = jnp.zeros_like(kprev_ref)
        vprev_ref[...] = jnp.zeros_like(vprev_ref)
        gbuf_ref[0:CONV_HALO, :] = jnp.zeros((CONV_HALO, CONV_CH), F32)

    c, sp, sm = c_ref[...], sp_ref[...], sm_ref[...]
    qn, kn = qn_ref[...], kn_ref[...]

    row = lax.broadcasted_iota(jnp.int32, (blk, 2 * blk), 0)
    col = lax.broadcasted_iota(jnp.int32, (blk, 2 * blk), 1)
    first_key = jnp.where(i > 0, 0, blk)
    mask = (col >= jnp.maximum(row, first_key)) & (col <= row + WINDOW)
    scale = HEAD_DIM ** -0.5
    for g in range(N_KV_HEADS):
        ks = slice(g * HEAD_DIM, (g + 1) * HEAD_DIM)
        k_cur = _norm_rope(kv_ref[:, ks], kn, c, sp, sm)
        v_cur = kv_ref[:, KV_COLS + g * HEAD_DIM:KV_COLS + (g + 1) * HEAD_DIM]
        nk_ref[0, :, ks] = k_cur
        nv_ref[0, :, ks] = v_cur
        k_cur_b = k_cur.astype(BF16)
        v_cur_b = v_cur.astype(BF16)
        k2 = jnp.concatenate([kprev_ref[:, ks], k_cur_b], axis=0)
        v2 = jnp.concatenate([vprev_ref[:, ks], v_cur_b], axis=0)
        for r in range(Q_PER_KV):
            h = g * Q_PER_KV + r
            hs = slice(h * HEAD_DIM, (h + 1) * HEAD_DIM)
            qh = _norm_rope(q_ref[:, hs], qn, c, sp, sm).astype(BF16)
            s = lax.dot_general(qh, k2, (((1,), (1,)), ((), ())),
                                preferred_element_type=F32) * scale
            s = jnp.where(mask, s, NEG_INF)
            sink = sinks_ref[h]
            m = jnp.maximum(jnp.max(s, axis=-1, keepdims=True), sink)
            p = jnp.exp(s - m)
            denom = jnp.sum(p, axis=-1, keepdims=True) + jnp.exp(sink - m)
            w = (p / denom).astype(BF16)
            o = jnp.dot(w, v2, preferred_element_type=F32)
            mix_ref[:, hs] = o.astype(mix_ref.dtype)
        kprev_ref[:, ks] = k_cur_b
        vprev_ref[:, ks] = v_cur_b

    half_ch = CONV_CH // 2
    for hh, (a_ref, g_ref) in enumerate(((a0_ref, g0_ref), (a1_ref, g1_ref))):
        cs = slice(hh * half_ch, (hh + 1) * half_ch)
        a = a_ref[...] + glub_ref[:, cs]
        gate = g_ref[...] + glub_ref[:, CONV_CH + hh * half_ch:CONV_CH + (hh + 1) * half_ch]
        gbuf_ref[CONV_HALO:CONV_HALO + blk, cs] = a * _sigmoid(gate)

    lane_chunk = V7X_LANES
    first = CONV_HALO - (CONV_WIDTH - 1)
    sub = V7X_SUBLANES

    def conv_chunk(ci, carry):
        ls = pl.ds(pl.multiple_of(ci * lane_chunk, lane_chunk), lane_chunk)
        acc = jnp.zeros((blk, lane_chunk), F32)
        for s in range(sub):
            taps = [w for w in range(CONV_WIDTH) if (first + w) % sub == s]
            rows = blk if s == 0 else blk + sub
            part = jnp.zeros((rows, lane_chunk), F32)
            for w in taps:
                base = first + w - s
                part = part + cw_ref[w:w + 1, ls] * gbuf_ref[base:base + rows, ls]
            acc = acc + part[s:s + blk, :]
        cbuf_ref[:, ls] = acc
        return carry

    lax.fori_loop(0, CONV_CH // lane_chunk, conv_chunk, 0)
    y = _layer_norm_silu(cbuf_ref[...] + cb_ref[...], lng_ref[...], lnb_ref[...])
    mix_ref[:, ATT_WIDTH:ATT_WIDTH + CONV_CH] = y.astype(mix_ref.dtype)

    @pl.when(i == SEQ // WINDOW - 1)
    def _emit_conv_state():
        nc_ref[0] = gbuf_ref[CONV_HALO + blk - (CONV_WIDTH - 1):CONV_HALO + blk, :]

    gbuf_ref[0:CONV_HALO, :] = gbuf_ref[blk:blk + CONV_HALO, :]


def _even_core_prompt(z, sample_tile, sinks, qn, kn, glub, cw, cb, lng, lnb):
    nb = SEQ // WINDOW
    blk = WINDOW
    c, sp, sm = _rope_tables(jnp.arange(SEQ, dtype=jnp.int32))
    last = BATCH * nb - 1
    rowmap = lambda col: (lambda s: (jnp.minimum(s, last), col))
    per_seq = lambda s: (jnp.minimum(s, last) // nb, 0, 0)
    tab = pl.BlockSpec((blk, HEAD_DIM), lambda s: (s % nb, 0))
    vec = lambda n: pl.BlockSpec((1, n), lambda s: (0, 0))
    est = (2 * blk * (2048 + 1024 + 4 * 1024) * 4 + 4 * blk * 4096 * 2 + 4 * blk * 512 * 4
           + (blk + CONV_HALO) * CONV_CH * 4 + 3 * blk * CONV_CH * 4 + 2 * CONV_WIDTH * CONV_CH * 4 + (8 << 20))
    return pl.pallas_call(
        _even_prompt_body,
        out_shape=(jax.ShapeDtypeStruct((M_ROWS, 2 * ATT_WIDTH), BF16),
                   jax.ShapeDtypeStruct((BATCH, blk, KV_COLS), F32),
                   jax.ShapeDtypeStruct((BATCH, blk, KV_COLS), F32),
                   jax.ShapeDtypeStruct((BATCH, CONV_WIDTH - 1, CONV_CH), F32)),
        grid=(BATCH * nb + 1,),
        in_specs=[pl.BlockSpec(memory_space=pltpu.SMEM),
                  pl.BlockSpec((MERGE_TILE, 2 * ATT_WIDTH), lambda s: (0, 0)),
                  pl.BlockSpec((blk, 2048), rowmap(0)),
                  pl.BlockSpec((blk, 1024), rowmap(2)),
                  pl.BlockSpec((blk, 1024), rowmap(3)),
                  pl.BlockSpec((blk, 1024), rowmap(4)),
                  pl.BlockSpec((blk, 1024), rowmap(5)),
                  pl.BlockSpec((blk, 1024), rowmap(6)),
                  tab, tab, tab, vec(HEAD_DIM), vec(HEAD_DIM), vec(2 * CONV_CH),
                  pl.BlockSpec((CONV_WIDTH, CONV_CH), lambda s: (0, 0)),
                  vec(CONV_CH), vec(CONV_CH), vec(CONV_CH)],
        out_specs=(pl.BlockSpec((blk, 2 * ATT_WIDTH), lambda s: (s, 0)),
                   pl.BlockSpec((1, blk, KV_COLS), per_seq),
                   pl.BlockSpec((1, blk, KV_COLS), per_seq),
                   pl.BlockSpec((1, CONV_WIDTH - 1, CONV_CH), per_seq)),
        scratch_shapes=[pltpu.VMEM((blk, KV_COLS), BF16), pltpu.VMEM((blk, KV_COLS), BF16),
                        pltpu.VMEM((blk + CONV_HALO, CONV_CH), F32), pltpu.VMEM((blk, CONV_CH), F32)],
        compiler_params=_cparams(1, est),
        name="even_core_prompt",
    )(sinks, sample_tile, z, z, z, z, z, z, c, sp, sm, qn.reshape(1, -1), kn.reshape(1, -1),
      glub.reshape(1, -1), cw, cb.reshape(1, -1), lng.reshape(1, -1), lnb.reshape(1, -1))


def _even_sample_body(sinks_ref, z_ref, ck_ref, cv_ref, sc_ref, c_ref, sp_ref, sm_ref, qn_ref, kn_ref,
                      glub_ref, cw_ref, cb_ref, lng_ref, lnb_ref,
                      mix_ref, nk_ref, nv_ref, nc_ref):
    zrow = lambda lo, hi: z_ref[0, :, lo:hi]
    c, sp, sm = c_ref[...], sp_ref[...], sm_ref[...]
    qn, kn = qn_ref[...], kn_ref[...]
    nbuf = WINDOW
    scale = HEAD_DIM ** -0.5
    qrow = lax.broadcasted_iota(jnp.int32, (V7X_SUBLANES, 1), 0)
    krow = lax.broadcasted_iota(jnp.int32, (nbuf, HEAD_DIM), 0)
    for g in range(N_KV_HEADS):
        ks = slice(g * HEAD_DIM, (g + 1) * HEAD_DIM)
        k_new = _norm_rope(zrow(Q_COLS + g * HEAD_DIM, Q_COLS + (g + 1) * HEAD_DIM), kn, c, sp, sm)
        v_new = zrow(Q_COLS + KV_COLS + g * HEAD_DIM, Q_COLS + KV_COLS + (g + 1) * HEAD_DIM)
        k_old = ck_ref[0, :, ks]
        v_old = cv_ref[0, :, ks]
        nk_ref[0, :, ks] = jnp.where(krow == nbuf - 1, k_new, pltpu.roll(k_old, nbuf - 1, 0))
        nv_ref[0, :, ks] = jnp.where(krow == nbuf - 1, v_new, pltpu.roll(v_old, nbuf - 1, 0))
        qs = [_norm_rope(zrow((g * Q_PER_KV + r) * HEAD_DIM, (g * Q_PER_KV + r + 1) * HEAD_DIM),
                         qn, c, sp, sm) for r in range(Q_PER_KV)]
        q8 = jnp.zeros((V7X_SUBLANES, HEAD_DIM), F32)
        for r in range(Q_PER_KV):
            q8 = jnp.where(qrow == r, qs[r], q8)
        q8b = q8.astype(BF16)
        s = lax.dot_general(q8b, k_old.astype(BF16), (((1,), (1,)), ((), ())),
                            preferred_element_type=F32) * scale
        k_new_r = k_new.astype(BF16).astype(F32)
        v_new_r = v_new.astype(BF16).astype(F32)
        s_self = jnp.sum(q8b.astype(F32) * k_new_r, axis=-1, keepdims=True) * scale
        sink = jnp.zeros((V7X_SUBLANES, 1), F32)
        for r in range(Q_PER_KV):
            sink = jnp.where(qrow == r, sinks_ref[g * Q_PER_KV + r], sink)
        m = jnp.maximum(jnp.maximum(jnp.max(s, axis=-1, keepdims=True), s_self), sink)
        p = jnp.exp(s - m)
        p_self = jnp.exp(s_self - m)
        denom = jnp.sum(p, axis=-1, keepdims=True) + p_self + jnp.exp(sink - m)
        o = jnp.dot((p / denom).astype(BF16), v_old.astype(BF16), preferred_element_type=F32)
        o = o + (p_self / denom).astype(BF16).astype(F32) * v_new_r
        for r in range(Q_PER_KV):
            h = g * Q_PER_KV + r
            mix_ref[0, :, h * HEAD_DIM:(h + 1) * HEAD_DIM] = o[r:r + 1, :]

    conv0 = Q_COLS + 2 * KV_COLS
    a = zrow(conv0, conv0 + CONV_CH) + glub_ref[:, 0:CONV_CH]
    gate = zrow(conv0 + CONV_CH, EVEN_IN) + glub_ref[:, CONV_CH:2 * CONV_CH]
    gl = a * _sigmoid(gate)
    nst = CONV_WIDTH - 1
    conv = jnp.sum(cw_ref[0:nst, :] * sc_ref[0], axis=0, keepdims=True) + cw_ref[nst:nst + 1, :] * gl
    y = _layer_norm_silu(conv + cb_ref[...], lng_ref[...], lnb_ref[...])
    mix_ref[0, :, ATT_WIDTH:ATT_WIDTH + CONV_CH] = y
    nc_ref[0, 0:nst - 1, :] = sc_ref[0, 1:nst, :]
    nc_ref[0, nst - 1:nst, :] = gl


def _even_core_sample(z, cache_k, cache_v, state_conv, sinks, qn, kn, glub, cw, cb, lng, lnb):
    nb = DEC_BATCH
    c, sp, sm = _rope_tables(jnp.full((1,), PAST_LEN, jnp.int32))
    vec = lambda n: pl.BlockSpec((1, n), lambda b: (0, 0))
    per_b = lambda r, n: pl.BlockSpec((1, r, n), lambda b: (b, 0, 0))
    nst = CONV_WIDTH - 1
    est = 2 * (EVEN_IN * 4 * nb + 4 * WINDOW * KV_COLS * 4 + 2 * 32 * CONV_CH * 4 + 4096 * 4 * 8) + (8 << 20)
    return pl.pallas_call(
        _even_sample_body,
        out_shape=(jax.ShapeDtypeStruct((nb, 1, 2 * ATT_WIDTH), F32),
                   jax.ShapeDtypeStruct((nb, WINDOW, KV_COLS), F32),
                   jax.ShapeDtypeStruct((nb, WINDOW, KV_COLS), F32),
                   jax.ShapeDtypeStruct((nb, nst, CONV_CH), F32)),
        grid=(nb,),
        in_specs=[pl.BlockSpec(memory_space=pltpu.SMEM),
                  per_b(1, EVEN_IN),
                  per_b(WINDOW, KV_COLS), per_b(WINDOW, KV_COLS), per_b(nst, CONV_CH),
                  vec(HEAD_DIM), vec(HEAD_DIM), vec(HEAD_DIM), vec(HEAD_DIM), vec(HEAD_DIM),
                  vec(2 * CONV_CH), pl.BlockSpec((CONV_WIDTH, CONV_CH), lambda b: (0, 0)),
                  vec(CONV_CH), vec(CONV_CH), vec(CONV_CH)],
        out_specs=(per_b(1, 2 * ATT_WIDTH), per_b(WINDOW, KV_COLS), per_b(WINDOW, KV_COLS),
                   per_b(nst, CONV_CH)),
        compiler_params=_cparams(1, est),
        name="even_core_sample",
    )(sinks, z[PROMPT_ROWS:PROMPT_ROWS + nb].reshape(nb, 1, EVEN_IN), cache_k, cache_v, state_conv, c, sp, sm,
      qn.reshape(1, -1), kn.reshape(1, -1), glub.reshape(1, -1), cw, cb.reshape(1, -1),
      lng.reshape(1, -1), lnb.reshape(1, -1))


def _sample_tile(sample_rows):
    pad = jnp.zeros((MERGE_TILE - DEC_BATCH, sample_rows.shape[1]), BF16)
    return jnp.concatenate([sample_rows.astype(BF16), pad], axis=0)


def _ssm_prep_body(lre_ref, lim_ref, ldt_ref, btre_ref, btim_ref,
                   are_ref, aim_ref, bbre_ref, bbim_ref, abre_ref, abim_ref, cre_ref, cim_ref):
    lre, lim = lre_ref[...], lim_ref[...]
    dt = jnp.exp(ldt_ref[...])
    mag = jnp.exp(lre * dt)
    ang = lim * dt
    are = mag * jnp.cos(ang)
    aim = mag * jnp.sin(ang)
    are_ref[...] = are
    aim_ref[...] = aim
    nre, nim = are - 1.0, aim
    den = lre * lre + lim * lim
    cre_ref[...] = (nre * lre + nim * lim) / den
    cim_ref[...] = (nim * lre - nre * lim) / den

    def per_group(g, carry):
        cr = cre_ref[pl.ds(g, 1), :]
        ci = cim_ref[pl.ds(g, 1), :]
        br, bi = btre_ref[g], btim_ref[g]
        bbr = cr * br - ci * bi
        bbi = cr * bi + ci * br
        bbre_ref[g] = bbr
        bbim_ref[g] = bbi
        ar = are_ref[pl.ds(g, 1), :]
        ai = aim_ref[pl.ds(g, 1), :]
        abre_ref[g] = ar * bbr - ai * bbi
        abim_ref[g] = ar * bbi + ai * bbr
        return carry

    lax.fori_loop(0, SSM_GROUPS, per_group, 0)


def _ssm_layout_body(bbre_ref, bbim_ref, abre_ref, abim_ref, cre_ref, cim_ref, b_ref, c_ref):
    n, ch, st = SSM_STATE, SSM_TILE_CH, SSM_TILE_ST
    p_shift = SSM_GROUP.bit_length() - 1
    n_shift = SSM_STATE.bit_length() - 1
    iota = lambda shape, d: lax.broadcasted_iota(jnp.int32, shape, d)
    spread = jnp.where((iota((n, st), 1) & (n - 1)) == iota((n, st), 0), 1.0, 0.0).astype(BF16)
    own_b = (iota((ch, st), 0) >> p_shift) == (iota((ch, st), 1) >> n_shift)
    for k, src_ref in enumerate((bbre_ref, bbim_ref, abre_ref, abim_ref)):
        rep = jnp.dot(src_ref[...].astype(BF16), spread, preferred_element_type=F32)
        r0, c0 = (k // 2) * ch, (k % 2) * st
        b_ref[0, r0:r0 + ch, c0:c0 + st] = jnp.where(own_b, rep, 0.0).astype(BF16)
    spread_t = jnp.where((iota((st, n), 0) & (n - 1)) == iota((st, n), 1), 1.0, 0.0).astype(BF16)
    own_c = (iota((st, ch), 0) >> n_shift) == (iota((st, ch), 1) >> p_shift)
    for k, (src_ref, sign) in enumerate(((cre_ref, 1.0), (cim_ref, -1.0))):
        rep = lax.dot_general(spread_t, src_ref[...].astype(BF16), (((1,), (1,)), ((), ())),
                              preferred_element_type=F32)
        c_ref[0, k * st:(k + 1) * st, :] = jnp.where(own_c, sign * rep, 0.0).astype(BF16)


def _ssm_params(lam_re, lam_im, log_dt, b_re, b_im, c_re, c_im, d_skip):
    g, n, p = SSM_GROUPS, SSM_STATE, SSM_GROUP
    bt_re = jnp.swapaxes(b_re, 1, 2)
    bt_im = jnp.swapaxes(b_im, 1, 2)
    gn = jax.ShapeDtypeStruct((g, n), F32)
    gpn = jax.ShapeDtypeStruct((g, p, n), F32)
    are, aim, bbre, bbim, abre, abim = pl.pallas_call(
        _ssm_prep_body,
        out_shape=(gn, gn, gpn, gpn, gpn, gpn),
        scratch_shapes=[pltpu.VMEM((g, n), F32), pltpu.VMEM((g, n), F32)],
        name="ssm_prep",
    )(lam_re, lam_im, log_dt.reshape(g, 1), bt_re, bt_im)

    t, tg = SSM_TILES, SSM_TILE_GROUPS
    ch, st = SSM_TILE_CH, SSM_TILE_ST
    rows_gp = lambda m: m.reshape(g * p, n)
    src_spec = pl.BlockSpec((ch, n), lambda i: (i, 0))
    b_mat, c_mat = pl.pallas_call(
        _ssm_layout_body,
        out_shape=(jax.ShapeDtypeStruct((t, 2 * ch, 2 * st), BF16),
                   jax.ShapeDtypeStruct((t, 2 * st, ch), BF16)),
        grid=(t,),
        in_specs=[src_spec] * 6,
        out_specs=(pl.BlockSpec((1, 2 * ch, 2 * st), lambda i: (i, 0, 0)),
                   pl.BlockSpec((1, 2 * st, ch), lambda i: (i, 0, 0))),
        compiler_params=_cparams(1, 2 * (2 * ch * 2 * st * 2 + 2 * st * ch * 2) + 8 * ch * st * 4),
        name="ssm_layout",
    )(rows_gp(bbre), rows_gp(bbim), rows_gp(abre), rows_gp(abim), rows_gp(c_re), rows_gp(c_im))
    return (b_mat, c_mat, are.reshape(t, 1, tg * n), aim.reshape(t, 1, tg * n),
            d_skip.reshape(t, 1, tg * p))


def _gelu_tanh(y):
    return 0.5 * y * (1.0 + jnp.tanh(math.sqrt(2.0 / math.pi) * (y + 0.044715 * (y * y * y))))


def _ssm_prompt_body(u0_ref, u1_ref, u2_ref, u3_ref, tile_ref, b_ref, c_ref, are_ref, aim_ref, d_ref,
                     y_ref, hre_ref, him_ref,
                     us_ref, ys_ref, bu_ref, h_ref, carry_ref, p_re_ref, p_im_ref):
    st = SSM_TILE_ST
    sub = V7X_SUBLANES
    nl = st // V7X_LANES
    nh = SSM_TILE_CH // V7X_LANES
    tt = pl.program_id(1)
    steps = u0_ref.shape[0]
    rows = steps * BATCH

    @pl.when(tt == 0)
    def _init():
        carry_ref[...] = jnp.zeros_like(carry_ref)
        ar = jnp.broadcast_to(are_ref[0], (sub, st))
        ai = jnp.broadcast_to(aim_ref[0], (sub, st))
        lo = lax.broadcasted_iota(jnp.int32, (sub, st), 0) < BATCH
        p_re_ref[...] = jnp.where(lo, ar, ar * ar - ai * ai)
        p_im_ref[...] = jnp.where(lo, ai, 2.0 * ar * ai)

    for b, ub_ref in enumerate((u0_ref, u1_ref, u2_ref, u3_ref)):
        for hh in range(nh):
            us_ref[hh, pl.ds(b, steps, stride=BATCH), :] = ub_ref[:, hh * V7X_LANES:(hh + 1) * V7X_LANES]
    u = jnp.concatenate([us_ref[hh] for hh in range(nh)], axis=1)
    second = (lax.broadcasted_iota(jnp.int32, u.shape, 0) % sub) >= BATCH
    u_prev = jnp.where(second, pltpu.roll(u, BATCH, 0), 0.0)
    u2 = jnp.concatenate([u, u_prev], axis=1).astype(BF16)
    half = rows // SSM_PARTS
    for hf in range(SSM_PARTS):
        bu_ref[hf] = jnp.dot(u2[hf * half:(hf + 1) * half], b_ref[0], preferred_element_type=F32)

    lo8 = lax.broadcasted_iota(jnp.int32, (sub, V7X_LANES), 0) < BATCH
    pack = 2 * sub

    def scan_half(hf, carry):
        for i in range(half // pack):
            r0 = i * pack
            new = []
            for j in range(nl):
                lre = slice(j * V7X_LANES, (j + 1) * V7X_LANES)
                lim = slice(st + j * V7X_LANES, st + (j + 1) * V7X_LANES)
                pr, pi = p_re_ref[:, lre], p_im_ref[:, lre]
                hr, hi = carry[2 * j], carry[2 * j + 1]
                outs_r, outs_i = [], []
                for k in range(2):
                    yr = bu_ref[hf, r0 + k * sub:r0 + (k + 1) * sub, lre]
                    yi = bu_ref[hf, r0 + k * sub:r0 + (k + 1) * sub, lim]
                    hbr = jnp.where(lo8, pltpu.roll(hr, BATCH, 0), hr)
                    hbi = jnp.where(lo8, pltpu.roll(hi, BATCH, 0), hi)
                    hr = yr + pr * hbr - pi * hbi
                    hi = yi + pr * hbi + pi * hbr
                    outs_r.append(hr)
                    outs_i.append(hi)
                h_ref[hf, r0:r0 + pack, lre] = jnp.concatenate(outs_r, axis=0).astype(h_ref.dtype)
                h_ref[hf, r0:r0 + pack, lim] = jnp.concatenate(outs_i, axis=0).astype(h_ref.dtype)
                new += [hr, hi]
            carry = tuple(new)
        return carry

    carry = []
    for j in range(nl):
        carry += [carry_ref[:, j * V7X_LANES:(j + 1) * V7X_LANES],
                  carry_ref[:, st + j * V7X_LANES:st + (j + 1) * V7X_LANES]]
    carry = tuple(carry)
    ys = []
    for hf in range(SSM_PARTS):
        carry = scan_half(hf, carry)
        ys.append(jnp.dot(h_ref[hf], c_ref[0], preferred_element_type=F32))
    for j in range(nl):
        carry_ref[:, j * V7X_LANES:(j + 1) * V7X_LANES] = carry[2 * j]
        carry_ref[:, st + j * V7X_LANES:st + (j + 1) * V7X_LANES] = carry[2 * j + 1]

    y = jnp.concatenate(ys, axis=0) + d_ref[0] * u
    y = _gelu_tanh(y)
    for hh in range(nh):
        ys_ref[hh] = y[:, hh * V7X_LANES:(hh + 1) * V7X_LANES]
    for b in range(BATCH):
        for hh in range(nh):
            y_ref[b, :, hh * V7X_LANES:(hh + 1) * V7X_LANES] = (
                ys_ref[hh, pl.ds(b, steps, stride=BATCH), :].astype(y_ref.dtype))
    y_ref[BATCH] = jnp.zeros((steps, SSM_TILE_CH), y_ref.dtype)

    @pl.when(tt == 0)
    def _sample_tile():
        y_ref[BATCH, 0:MERGE_TILE, :] = tile_ref[...]

    hre_ref[...] = carry_ref[:, 0:st]
    him_ref[...] = carry_ref[:, st:2 * st]


def _ssm_prompt(u, sample_tile, b_mat, c_mat, are, aim, d):
    steps = SSM_STEPS
    rows = steps * BATCH
    nt = SEQ // steps
    st, ch = SSM_TILE_ST, SSM_TILE_CH
    nh = ch // V7X_LANES
    est = (8 * steps * ch * 4 + 2 * BATCH * steps * ch * 2 + 6 * ch * 2 * st * 2 + 2 * rows * 2 * st * 4
           + rows * 2 * st * 2 + 4 * rows * ch * 4 + 8 * 8 * st * 4 + (4 << 20))
    u_spec = lambda b: pl.BlockSpec((steps, ch), lambda g, t, b=b: (b * nt + t, g))
    return pl.pallas_call(
        _ssm_prompt_body,
        out_shape=(jax.ShapeDtypeStruct((BATCH + 1, SEQ, D_MODEL), BF16),
                   jax.ShapeDtypeStruct((V7X_SUBLANES, SSM_GROUPS * SSM_STATE), F32),
                   jax.ShapeDtypeStruct((V7X_SUBLANES, SSM_GROUPS * SSM_STATE), F32)),
        grid=(SSM_TILES, nt),
        in_specs=[u_spec(0), u_spec(1), u_spec(2), u_spec(3),
                  pl.BlockSpec((MERGE_TILE, ch), lambda g, t: (0, g)),
                  pl.BlockSpec((1, 2 * ch, 2 * st), lambda g, t: (g, 0, 0)),
                  pl.BlockSpec((1, 2 * st, ch), lambda g, t: (g, 0, 0)),
                  pl.BlockSpec((1, 1, st), lambda g, t: (g, 0, 0)),
                  pl.BlockSpec((1, 1, st), lambda g, t: (g, 0, 0)),
                  pl.BlockSpec((1, 1, ch), lambda g, t: (g, 0, 0))],
        out_specs=(pl.BlockSpec((BATCH + 1, steps, ch), lambda g, t: (0, t, g)),
                   pl.BlockSpec((V7X_SUBLANES, st), lambda g, t: (0, g)),
                   pl.BlockSpec((V7X_SUBLANES, st), lambda g, t: (0, g))),
        scratch_shapes=[pltpu.VMEM((nh, rows, V7X_LANES), F32), pltpu.VMEM((nh, rows, V7X_LANES), F32),
                        pltpu.VMEM((SSM_PARTS, rows // SSM_PARTS, 2 * st), F32),
                        pltpu.VMEM((SSM_PARTS, rows // SSM_PARTS, 2 * st), BF16),
                        pltpu.VMEM((V7X_SUBLANES, 2 * st), F32),
                        pltpu.VMEM((V7X_SUBLANES, st), F32), pltpu.VMEM((V7X_SUBLANES, st), F32)],
        compiler_params=_cparams(2, est),
        name="ssm_prompt",
    )(u, u, u, u, sample_tile, b_mat, c_mat, are, aim, d)


def _ssm_sample_body(u_ref, h0re_ref, h0im_ref, b_ref, c_ref, are_ref, aim_ref, d_ref,
                     y_ref, hre_ref, him_ref):
    st = SSM_TILE_ST
    u = u_ref[...]
    bu = jnp.dot(u.astype(BF16), b_ref[0], preferred_element_type=F32)
    ar, ai = are_ref[0], aim_ref[0]
    h0r, h0i = h0re_ref[...], h0im_ref[...]
    hr = ar * h0r - ai * h0i + bu[:, 0:st]
    hi = ar * h0i + ai * h0r + bu[:, st:2 * st]
    hre_ref[...] = hr
    him_ref[...] = hi
    hcat = jnp.concatenate([hr, hi], axis=1).astype(BF16)
    y = jnp.dot(hcat, c_ref[0], preferred_element_type=F32) + d_ref[0] * u
    y_ref[...] = _gelu_tanh(y).astype(y_ref.dtype)


def _ssm_sample(u, h0_re, h0_im, b_mat, c_mat, are, aim, d):
    nb = DEC_BATCH
    st, ch = SSM_TILE_ST, SSM_TILE_CH
    est = 2 * (nb * ch * 6 + 4 * nb * st * 4 + 2 * ch * 2 * st * 2) + (4 << 20)
    return pl.pallas_call(
        _ssm_sample_body,
        out_shape=(jax.ShapeDtypeStruct((nb, D_MODEL), BF16),
                   jax.ShapeDtypeStruct((nb, SSM_GROUPS * SSM_STATE), F32),
                   jax.ShapeDtypeStruct((nb, SSM_GROUPS * SSM_STATE), F32)),
        grid=(SSM_TILES,),
        in_specs=[pl.BlockSpec((nb, ch), lambda g: (PROMPT_ROWS // DEC_BATCH, g)),
                  pl.BlockSpec((nb, st), lambda g: (0, g)),
                  pl.BlockSpec((nb, st), lambda g: (0, g)),
                  pl.BlockSpec((1, ch, 2 * st), lambda g: (g, 0, 0)),
                  pl.BlockSpec((1, 2 * st, ch), lambda g: (g, 0, 0)),
                  pl.BlockSpec((1, 1, st), lambda g: (g, 0, 0)),
                  pl.BlockSpec((1, 1, st), lambda g: (g, 0, 0)),
                  pl.BlockSpec((1, 1, ch), lambda g: (g, 0, 0))],
        out_specs=(pl.BlockSpec((nb, ch), lambda g: (0, g)),
                   pl.BlockSpec((nb, st), lambda g: (0, g)),
                   pl.BlockSpec((nb, st), lambda g: (0, g))),
        compiler_params=_cparams(1, est),
        name="ssm_sample",
    )(u, h0_re, h0_im, b_mat, c_mat, are, aim, d)


def kernel(x_prompt, x_sample, cache_swa_k, cache_swa_v, state_conv, state_ssm_re, state_ssm_im, ffn_norm, ffn_w_gu, ffn_w_down, mix_norm, even_w_in, even_q_norm, even_k_norm, even_sinks, even_glu_b, even_conv_w, even_conv_b, even_ln_g, even_ln_b, even_w_out, odd_w_in, odd_lam_re, odd_lam_im, odd_log_dt, odd_b_re, odd_b_im, odd_c_re, odd_c_im, odd_d, odd_w_gate, odd_w_out):
    nb = DEC_BATCH
    ssm = _ssm_params(odd_lam_re[0], odd_lam_im[0], odd_log_dt[0], odd_b_re[0], odd_b_im[0],
                      odd_c_re[0], odd_c_im[0], odd_d[0])

    x, h = _merge_rmsnorm(x_prompt.reshape(PROMPT_ROWS, D_MODEL), x_sample.reshape(nb, D_MODEL),
                          ffn_norm[0, 0])
    x, gu_next = _half_ffn(x, h, ffn_w_gu, ffn_w_down, (0, 0), next_gu=(0, 1))

    even_p = (even_sinks[0], even_q_norm[0], even_k_norm[0], even_glu_b[0], even_conv_w[0],
              even_conv_b[0], even_ln_g[0], even_ln_b[0])
    z = _matmul(_rmsnorm(x, mix_norm[0]), even_w_in, (0,), n_cols=EVEN_IN, name="even_in")
    mix_s, sk, sv, sc = _even_core_sample(
        z, cache_swa_k.reshape(nb, WINDOW, KV_COLS), cache_swa_v.reshape(nb, WINDOW, KV_COLS),
        state_conv.reshape(nb, CONV_WIDTH - 1, CONV_CH), *even_p)
    mix, pk, pv, pc = _even_core_prompt(z, _sample_tile(mix_s.reshape(nb, D_MODEL)), *even_p)
    x = _matmul(mix, even_w_out, (0,), n_cols=D_MODEL, epilogue="res", extra=x, name="even_out")
    x, gu_next = _half_ffn(x, _rmsnorm(x, ffn_norm[0, 1]), ffn_w_gu, ffn_w_down, (0, 1),
                           gu_bf16=gu_next, next_gu=(1, 0))

    x, gu_next = _half_ffn(x, _rmsnorm(x, ffn_norm[1, 0]), ffn_w_gu, ffn_w_down, (1, 0),
                           gu_bf16=gu_next, next_gu=(1, 1))
    u = _matmul(_rmsnorm(x, mix_norm[1]), odd_w_in, (0,), n_cols=D_MODEL, name="odd_in")
    y_s, sre, sim = _ssm_sample(u, state_ssm_re.reshape(nb, -1), state_ssm_im.reshape(nb, -1), *ssm)
    y3, pre8, pim8 = _ssm_prompt(u, _sample_tile(y_s), *ssm)
    y = y3.reshape((BATCH + 1) * SEQ, D_MODEL)
    yg = _matmul(y, odd_w_gate, (0,), n_cols=D_MODEL, epilogue="gate", out_dtype=BF16, extra=y,
                 name="odd_gate")
    x = _matmul(yg, odd_w_out, (0,), n_cols=D_MODEL, epilogue="res", extra=x, name="odd_out")
    (y_p, y_smp), _ = _half_ffn(x, _rmsnorm(x, ffn_norm[1, 1]), ffn_w_gu, ffn_w_down, (1, 1),
                                gu_bf16=gu_next, split_out=True)

    kv5 = lambda t, b: t.reshape(1, b, WINDOW, N_KV_HEADS, HEAD_DIM)
    st4 = lambda t, b: t.reshape(1, b, SSM_GROUPS, SSM_STATE)
    return (y_p.reshape(BATCH, SEQ, D_MODEL), y_smp.reshape(nb, 1, D_MODEL),
            kv5(pk, BATCH), kv5(pv, BATCH), pc.reshape(1, BATCH, CONV_WIDTH - 1, CONV_CH),
            st4(pre8[BATCH:2 * BATCH], BATCH), st4(pim8[BATCH:2 * BATCH], BATCH),
            kv5(sk, nb), kv5(sv, nb), sc.reshape(1, nb, CONV_WIDTH - 1, CONV_CH),
            st4(sre, nb), st4(sim, nb))
```

```python
import functools
import math

import jax
import jax.numpy as jnp
from jax import lax
from jax.experimental import pallas as pl
from jax.experimental.pallas import tpu as pltpu

F32 = jnp.float32
BF16 = jnp.bfloat16

D_MODEL = 4096
BATCH = 4
SEQ = 2048
DEC_BATCH = 32
PAST_LEN = 16384
HEAD_DIM = 128
N_HEADS = 16
N_KV_HEADS = 4
Q_PER_KV = 4
WINDOW = 128
ROT_DIM = 32
ROPE_THETA = 500000.0
ATT_WIDTH = 2048
CONV_CH = 2048
CONV_WIDTH = 31
Q_COLS = 2048
KV_COLS = 512
EVEN_IN = 7168
SSM_GROUPS = 256
SSM_GROUP = 16
SSM_STATE = 64
D_FF = 11008
EPS = 1e-6
NEG_INF = -1e30

V7X_VMEM_BYTES = 64 * 1024 * 1024
V7X_LANES = 128
V7X_SUBLANES = 8
VMEM_CAP = V7X_VMEM_BYTES - 3 * 1024 * 1024

PROMPT_ROWS = BATCH * SEQ
MERGE_TILE = 128
M_ROWS = PROMPT_ROWS + MERGE_TILE
ROW_TILES = 8
TM = M_ROWS // ROW_TILES
TM_DOWN = TM // 2

SSM_TILE_GROUPS = 16
SSM_TILE_CH = SSM_TILE_GROUPS * SSM_GROUP
SSM_TILE_ST = SSM_TILE_GROUPS * SSM_STATE
SSM_TILES = SSM_GROUPS // SSM_TILE_GROUPS
SSM_STEPS = 256
SSM_PARTS = 4
CONV_HALO = 32


def _cparams(n_grid, est_bytes):
    limit = int(min(VMEM_CAP, est_bytes * 5 // 4 + (4 << 20)))
    return pltpu.CompilerParams(dimension_semantics=("arbitrary",) * n_grid,
                                vmem_limit_bytes=limit)


def _sigmoid(x):
    return 0.5 * (1.0 + jnp.tanh(0.5 * x))


def _rms(x, g):
    ms = jnp.mean(x * x, axis=-1, keepdims=True)
    return x * lax.rsqrt(ms + EPS) * g


def _rmsnorm_body(x_ref, g_ref, o_ref):
    o_ref[...] = _rms(x_ref[...], g_ref[...]).astype(o_ref.dtype)


def _rmsnorm(x, g):
    d = x.shape[1]
    tr = 2 * TM // 5
    return pl.pallas_call(
        _rmsnorm_body,
        out_shape=jax.ShapeDtypeStruct((M_ROWS, d), BF16),
        grid=(M_ROWS // tr,),
        in_specs=[pl.BlockSpec((tr, d), lambda i: (i, 0)),
                  pl.BlockSpec((1, d), lambda i: (0, 0))],
        out_specs=pl.BlockSpec((tr, d), lambda i: (i, 0)),
        compiler_params=_cparams(1, 2 * tr * d * 4 + 2 * tr * d * 2),
        name="rmsnorm",
    )(x, g.reshape(1, d))


def _merge_rmsnorm_body(xp_ref, xs_ref, g_ref, x_ref, h_ref):
    i = pl.program_id(0)
    g = g_ref[...]

    @pl.when(i < PROMPT_ROWS // MERGE_TILE)
    def _prompt():
        x = xp_ref[...]
        x_ref[...] = x
        h_ref[...] = _rms(x, g).astype(h_ref.dtype)

    @pl.when(i == PROMPT_ROWS // MERGE_TILE)
    def _sample():
        xs = xs_ref[...]
        pad = MERGE_TILE - DEC_BATCH
        x_ref[0:DEC_BATCH, :] = xs
        x_ref[DEC_BATCH:MERGE_TILE, :] = jnp.zeros((pad, D_MODEL), F32)
        h_ref[0:DEC_BATCH, :] = _rms(xs, g).astype(h_ref.dtype)
        h_ref[DEC_BATCH:MERGE_TILE, :] = jnp.zeros((pad, D_MODEL), h_ref.dtype)


def _merge_rmsnorm(xp, xs, g):
    d = D_MODEL
    last = PROMPT_ROWS // MERGE_TILE - 1
    return pl.pallas_call(
        _merge_rmsnorm_body,
        out_shape=(jax.ShapeDtypeStruct((M_ROWS, d), F32), jax.ShapeDtypeStruct((M_ROWS, d), BF16)),
        grid=(M_ROWS // MERGE_TILE,),
        in_specs=[pl.BlockSpec((MERGE_TILE, d), lambda i: (jnp.minimum(i, last), 0)),
                  pl.BlockSpec((DEC_BATCH, d), lambda i: (0, 0)),
                  pl.BlockSpec((1, d), lambda i: (0, 0))],
        out_specs=(pl.BlockSpec((MERGE_TILE, d), lambda i: (i, 0)),
                   pl.BlockSpec((MERGE_TILE, d), lambda i: (i, 0))),
        compiler_params=_cparams(1, 4 * MERGE_TILE * d * 4 + 2 * MERGE_TILE * d * 2 + 2 * DEC_BATCH * d * 4),
        name="merge_rmsnorm",
    )(xp, xs, g.reshape(1, d))


def _mm_body(*refs, n_w, cast_w, epilogue, scale, side_cast, split_rows):
    a_ref = refs[0]
    w_refs = refs[1:1 + n_w]
    pos = 1 + n_w
    extra_ref = side_in_ref = side_out_ref = o2_ref = None
    if epilogue in ("res", "gate"):
        extra_ref = refs[pos]
        pos += 1
    if side_cast:
        side_in_ref = refs[pos]
        pos += 1
    o_ref = refs[pos]
    pos += 1
    if side_cast:
        side_out_ref = refs[pos]
        pos += 1
    if split_rows is not None:
        o2_ref = refs[pos]
        pos += 1
    tn = w_refs[0].shape[-1]
    if cast_w:
        wb_ref = refs[pos]

        @pl.when(pl.program_id(1) == 0)
        def _cast_weights():
            for n, w_ref in enumerate(w_refs):
                wb_ref[:, n * tn:(n + 1) * tn] = w_ref[...].astype(BF16)
    if side_cast:
        side_out_ref[...] = side_in_ref[...].astype(side_out_ref.dtype)

    a = a_ref[...].astype(BF16)
    if cast_w:
        acc = jnp.dot(a, wb_ref[...], preferred_element_type=F32)
        accs = [acc[:, n * tn:(n + 1) * tn] for n in range(n_w)]
    else:
        accs = [jnp.dot(a, w_ref[...], preferred_element_type=F32) for w_ref in w_refs]
    acc = accs[0]
    if epilogue == "plain":
        out = acc
    elif epilogue == "swiglu":
        gate, up = accs
        out = gate * _sigmoid(gate) * up
    elif epilogue == "res":
        out = extra_ref[...] + scale * acc
    elif epilogue == "gate":
        out = extra_ref[...].astype(F32) * _sigmoid(acc)
    o_ref[...] = out.astype(o_ref.dtype)

    if split_rows is not None:
        @pl.when(pl.program_id(1) == pl.num_programs(1) - 1)
        def _sample_rows():
            o2_ref[...] = out[split_rows:split_rows + DEC_BATCH, :].astype(o2_ref.dtype)


def _matmul(a, w, widx=(), *, n_cols, tm=TM, tn=512, epilogue="plain", out_dtype=F32, col_offs=(0,),
            extra=None, scale=1.0, side=None, split_out=False, name="mm"):
    k = a.shape[1]
    n_w = len(col_offs)
    cast_w = w.dtype != BF16
    nlead = len(widx)
    n_i = M_ROWS // tm
    n_j = n_cols // tn

    operands = [a]
    in_specs = [pl.BlockSpec((tm, k), lambda j, i: (i, 0))]
    for off in col_offs:
        operands.append(w)
        in_specs.append(pl.BlockSpec((None,) * nlead + (k, tn),
                                     lambda j, i, off=off: tuple(widx) + (0, j + off)))
    w_bytes = w.dtype.itemsize
    est = 2 * tm * k * a.dtype.itemsize + n_w * (2 * k * tn * w_bytes + (k * tn * 2 if cast_w else 0))
    if extra is not None:
        operands.append(extra)
        in_specs.append(pl.BlockSpec((tm, tn), lambda j, i: (i, j)))
        est += 2 * tm * tn * extra.dtype.itemsize
    out_shapes, out_specs = [], []
    if split_out:
        out_shapes.append(jax.ShapeDtypeStruct((PROMPT_ROWS, n_cols), out_dtype))
    else:
        out_shapes.append(jax.ShapeDtypeStruct((M_ROWS, n_cols), out_dtype))
    out_specs.append(pl.BlockSpec((tm, tn), lambda j, i: (i, j)))
    if side is not None:
        w_src, widx_src, rows_total = side
        slab = rows_total // (n_i * n_j)
        assert slab * n_i * n_j == rows_total and slab % 16 == 0
        cols_src = w_src.shape[-1]
        operands.append(w_src)
        in_specs.append(pl.BlockSpec((None,) * len(widx_src) + (slab, cols_src),
                                     lambda j, i: tuple(widx_src) + (j * n_i + i, 0)))
        out_shapes.append(jax.ShapeDtypeStruct((rows_total, cols_src), BF16))
        out_specs.append(pl.BlockSpec((slab, cols_src), lambda j, i: (j * n_i + i, 0)))
        est += 2 * slab * cols_src * 6
    split_rows = None
    if split_out:
        split_rows = PROMPT_ROWS - (n_i - 1) * tm
        assert split_rows % V7X_SUBLANES == 0 and 0 <= split_rows <= tm - DEC_BATCH
        out_shapes.append(jax.ShapeDtypeStruct((DEC_BATCH, n_cols), out_dtype))
        out_specs.append(pl.BlockSpec((DEC_BATCH, tn), lambda j, i: (0, j)))
    est += 2 * tm * tn * jnp.dtype(out_dtype).itemsize + (n_w + 1) * tm * tn * 4

    outs = pl.pallas_call(
        functools.partial(_mm_body, n_w=n_w, cast_w=cast_w, epilogue=epilogue, scale=scale,
                          side_cast=side is not None, split_rows=split_rows),
        out_shape=tuple(out_shapes),
        grid=(n_j, n_i),
        in_specs=in_specs,
        out_specs=tuple(out_specs),
        scratch_shapes=[pltpu.VMEM((k, n_w * tn), BF16)] if cast_w else [],
        compiler_params=_cparams(2, est),
        name=name,
    )(*operands)
    return outs[0] if len(outs) == 1 else outs


def _half_ffn(x, h, w_gu, w_down, widx, *, gu_bf16=None, next_gu=None, split_out=False):
    if gu_bf16 is None:
        gu_w, gu_idx, gu_tm = w_gu, widx, TM
    else:
        gu_w, gu_idx, gu_tm = gu_bf16, (), 2 * TM
    act, wd_bf16 = _matmul(h, gu_w, gu_idx, n_cols=D_FF, tm=gu_tm, tn=256, epilogue="swiglu",
                           out_dtype=BF16, col_offs=(0, D_FF // 256), side=(w_down, widx, D_FF),
                           name="ffn_gu")
    side = None if next_gu is None else (w_gu, next_gu, D_MODEL)
    outs = _matmul(act, wd_bf16, n_cols=D_MODEL, tm=TM_DOWN, tn=512, epilogue="res", extra=x, scale=0.5,
                   side=side, split_out=split_out, name="ffn_down")
    if next_gu is None:
        return outs, None
    return outs[0], outs[1]


def _rope_tables(pos):
    half = ROT_DIM // 2
    inv_freq = jnp.power(jnp.float32(ROPE_THETA), -jnp.arange(half, dtype=F32) * (2.0 / ROT_DIM))
    ang = pos.astype(F32)[:, None] * inv_freq[None, :]
    cos, sin = jnp.cos(ang), jnp.sin(ang)
    n = pos.shape[0]
    rest = HEAD_DIM - ROT_DIM
    c = jnp.concatenate([cos, cos, jnp.ones((n, rest), F32)], axis=1)
    sp = jnp.concatenate([jnp.zeros((n, half), F32), sin, jnp.zeros((n, rest), F32)], axis=1)
    sm = jnp.concatenate([-sin, jnp.zeros((n, HEAD_DIM - half), F32)], axis=1)
    return c, sp, sm


def _norm_rope(x, g, c, sp, sm):
    half = ROT_DIM // 2
    y = x * lax.rsqrt(jnp.mean(x * x, axis=-1, keepdims=True) + EPS) * g
    return y * c + pltpu.roll(y, half, 1) * sp + pltpu.roll(y, HEAD_DIM - half, 1) * sm


def _layer_norm_silu(c, g, b):
    mu = jnp.mean(c, axis=-1, keepdims=True)
    xc = c - mu
    var = jnp.mean(xc * xc, axis=-1, keepdims=True)
    y = xc * lax.rsqrt(var + EPS) * g + b
    return y * _sigmoid(y)


def _even_prompt_body(sinks_ref, tile_ref, *refs):
    nblk = BATCH * (SEQ // WINDOW)
    s = pl.program_id(0)

    @pl.when(s < nblk)
    def _prompt_block():
        _even_prompt_block(s % (SEQ // WINDOW), sinks_ref, *refs)

    @pl.when(s == nblk)
    def _sample_tile():
        mix_ref = refs[16]
        mix_ref[...] = tile_ref[...]


def _even_prompt_block(i, sinks_ref, q_ref, kv_ref, a0_ref, a1_ref, g0_ref, g1_ref,
                       c_ref, sp_ref, sm_ref, qn_ref, kn_ref, glub_ref, cw_ref, cb_ref,
                       lng_ref, lnb_ref,
                       mix_ref, nk_ref, nv_ref, nc_ref,
                       kprev_ref, vprev_ref, gbuf_ref, cbuf_ref,
                       q4_ref, k2_ref, v2_ref, obuf_ref, cwbuf_ref, zraw_ref, kf_ref):
    blk = WINDOW

    @pl.when(i == 0)
    def _reset():
        kprev_ref[...] = jnp.zeros_like(kprev_ref)
        vprev_ref[...] = jnp.zeros_like(vprev_ref)
        gbuf_ref[0:CONV_HALO, :] = jnp.zeros((CONV_HALO, CONV_CH), F32)
        cwbuf_ref[0:CONV_WIDTH, :] = cw_ref[...]

    c, sp, sm = c_ref[...], sp_ref[...], sm_ref[...]
    qn, kn = qn_ref[...], kn_ref[...]
    glub = glub_ref[...]

    half_ch = CONV_CH // 2
    conv0 = Q_COLS + 2 * KV_COLS
    zraw_ref[:, 0:Q_COLS] = q_ref[...]
    zraw_ref[:, Q_COLS:conv0] = kv_ref[...]
    for n, part_ref in enumerate((a0_ref, a1_ref, g0_ref, g1_ref)):
        zraw_ref[:, conv0 + n * half_ch:conv0 + (n + 1) * half_ch] = part_ref[...]

    a = zraw_ref[:, conv0:conv0 + CONV_CH] + glub[:, 0:CONV_CH]
    gate = zraw_ref[:, conv0 + CONV_CH:EVEN_IN] + glub[:, CONV_CH:2 * CONV_CH]
    gbuf_ref[CONV_HALO:CONV_HALO + blk, :] = a * _sigmoid(gate)

    lane_chunk = V7X_LANES
    first = CONV_HALO - (CONV_WIDTH - 1)
    sub = V7X_SUBLANES

    def conv_chunk(ci):
        ls = slice(ci * lane_chunk, (ci + 1) * lane_chunk)
        acc = jnp.zeros((blk, lane_chunk), F32)
        for s in range(sub):
            taps = [w for w in range(CONV_WIDTH) if (first + w) % sub == s]
            rows = blk if s == 0 else blk + sub
            part = jnp.zeros((rows, lane_chunk), F32)
            for w in taps:
                base = first + w - s
                part = part + cwbuf_ref[w:w + 1, ls] * gbuf_ref[base:base + rows, ls]
            acc = acc + part[s:s + blk, :]
        cbuf_ref[:, ls] = acc

    assert CONV_CH // lane_chunk == N_HEADS
    row = lax.broadcasted_iota(jnp.int32, (Q_PER_KV * blk, 2 * blk), 0) & (blk - 1)
    col = lax.broadcasted_iota(jnp.int32, (Q_PER_KV * blk, 2 * blk), 1)
    first_key = jnp.where(i > 0, 0, blk)
    mask4 = (col >= jnp.maximum(row, first_key)) & (col <= row + WINDOW)
    scale = HEAD_DIM ** -0.5
    for g in range(N_KV_HEADS):
        ks = slice(g * HEAD_DIM, (g + 1) * HEAD_DIM)
        k_cur = _norm_rope(zraw_ref[:, Q_COLS + g * HEAD_DIM:Q_COLS + (g + 1) * HEAD_DIM], kn, c, sp, sm)
        v_cur = zraw_ref[:, Q_COLS + KV_COLS + g * HEAD_DIM:Q_COLS + KV_COLS + (g + 1) * HEAD_DIM]
        kf_ref[:, ks] = k_cur
        k_cur_b = k_cur.astype(BF16)
        v_cur_b = v_cur.astype(BF16)
        k2_ref[g, 0:blk, :] = kprev_ref[:, ks]
        k2_ref[g, blk:2 * blk, :] = k_cur_b
        v2_ref[g, 0:blk, :] = vprev_ref[:, ks]
        v2_ref[g, blk:2 * blk, :] = v_cur_b
        kprev_ref[:, ks] = k_cur_b
        vprev_ref[:, ks] = v_cur_b
        for r in range(Q_PER_KV):
            h = g * Q_PER_KV + r
            q4_ref[g, r * blk:(r + 1) * blk, :] = _norm_rope(
                zraw_ref[:, h * HEAD_DIM:(h + 1) * HEAD_DIM], qn, c, sp, sm).astype(BF16)

    for g in range(N_KV_HEADS):
        heads = [g * Q_PER_KV + r for r in range(Q_PER_KV)]
        sink = jnp.concatenate([jnp.full((blk, 1), sinks_ref[h], F32) for h in heads], axis=0)
        s = lax.dot_general(q4_ref[g], k2_ref[g], (((1,), (1,)), ((), ())),
                            preferred_element_type=F32) * scale
        s = jnp.where(mask4, s, NEG_INF)
        m = jnp.maximum(jnp.max(s, axis=-1, keepdims=True), sink)
        p = jnp.exp(s - m)
        denom = jnp.sum(p, axis=-1, keepdims=True) + jnp.exp(sink - m)
        w = (p / denom).astype(BF16)
        o = jnp.dot(w, v2_ref[g], preferred_element_type=F32)
        for r, h in enumerate(heads):
            obuf_ref[:, h * HEAD_DIM:(h + 1) * HEAD_DIM] = o[r * blk:(r + 1) * blk, :].astype(obuf_ref.dtype)
            conv_chunk(h)

    nk_ref[0] = kf_ref[...]
    nv_ref[0] = zraw_ref[:, Q_COLS + KV_COLS:conv0]
    mix_ref[:, 0:ATT_WIDTH] = obuf_ref[...]
    y = _layer_norm_silu(cbuf_ref[...] + cb_ref[...], lng_ref[...], lnb_ref[...])
    mix_ref[:, ATT_WIDTH:ATT_WIDTH + CONV_CH] = y.astype(mix_ref.dtype)

    @pl.when(i == SEQ // WINDOW - 1)
    def _emit_conv_state():
        nc_ref[0] = gbuf_ref[CONV_HALO + blk - (CONV_WIDTH - 1):CONV_HALO + blk, :]

    gbuf_ref[0:CONV_HALO, :] = gbuf_ref[blk:blk + CONV_HALO, :]


def _even_core_prompt(z, sample_tile, sinks, qn, kn, glub, cw, cb, lng, lnb):
    nb = SEQ // WINDOW
    blk = WINDOW
    c, sp, sm = _rope_tables(jnp.arange(SEQ, dtype=jnp.int32))
    last = BATCH * nb - 1
    rowmap = lambda col: (lambda s: (jnp.minimum(s, last), col))
    per_seq = lambda s: (jnp.minimum(s, last) // nb, 0, 0)
    tab = pl.BlockSpec((blk, HEAD_DIM), lambda s: (s % nb, 0))
    vec = lambda n: pl.BlockSpec((1, n), lambda s: (0, 0))
    est = (2 * blk * (2048 + 1024 + 4 * 1024) * 4 + 4 * blk * 4096 * 2 + 4 * blk * 512 * 4
           + (blk + CONV_HALO) * CONV_CH * 4 + 3 * blk * CONV_CH * 4 + 2 * CONV_WIDTH * CONV_CH * 4 + (8 << 20))
    return pl.pallas_call(
        _even_prompt_body,
        out_shape=(jax.ShapeDtypeStruct((M_ROWS, 2 * ATT_WIDTH), BF16),
                   jax.ShapeDtypeStruct((BATCH, blk, KV_COLS), F32),
                   jax.ShapeDtypeStruct((BATCH, blk, KV_COLS), F32),
                   jax.ShapeDtypeStruct((BATCH, CONV_WIDTH - 1, CONV_CH), F32)),
        grid=(BATCH * nb + 1,),
        in_specs=[pl.BlockSpec(memory_space=pltpu.SMEM),
                  pl.BlockSpec((MERGE_TILE, 2 * ATT_WIDTH), lambda s: (0, 0)),
                  pl.BlockSpec((blk, 2048), rowmap(0)),
                  pl.BlockSpec((blk, 1024), rowmap(2)),
                  pl.BlockSpec((blk, 1024), rowmap(3)),
                  pl.BlockSpec((blk, 1024), rowmap(4)),
                  pl.BlockSpec((blk, 1024), rowmap(5)),
                  pl.BlockSpec((blk, 1024), rowmap(6)),
                  tab, tab, tab, vec(HEAD_DIM), vec(HEAD_DIM), vec(2 * CONV_CH),
                  pl.BlockSpec((CONV_WIDTH, CONV_CH), lambda s: (0, 0)),
                  vec(CONV_CH), vec(CONV_CH), vec(CONV_CH)],
        out_specs=(pl.BlockSpec((blk, 2 * ATT_WIDTH), lambda s: (s, 0)),
                   pl.BlockSpec((1, blk, KV_COLS), per_seq),
                   pl.BlockSpec((1, blk, KV_COLS), per_seq),
                   pl.BlockSpec((1, CONV_WIDTH - 1, CONV_CH), per_seq)),
        scratch_shapes=[pltpu.VMEM((blk, KV_COLS), BF16), pltpu.VMEM((blk, KV_COLS), BF16),
                        pltpu.VMEM((blk + CONV_HALO, CONV_CH), F32), pltpu.VMEM((blk, CONV_CH), F32),
                        pltpu.VMEM((N_KV_HEADS, Q_PER_KV * blk, HEAD_DIM), BF16),
                        pltpu.VMEM((N_KV_HEADS, 2 * blk, HEAD_DIM), BF16),
                        pltpu.VMEM((N_KV_HEADS, 2 * blk, HEAD_DIM), BF16),
                        pltpu.VMEM((blk, ATT_WIDTH), BF16),
                        pltpu.VMEM((CONV_HALO, CONV_CH), F32),
                        pltpu.VMEM((blk, EVEN_IN), F32), pltpu.VMEM((blk, KV_COLS), F32)],
        compiler_params=_cparams(1, est),
        name="even_core_prompt",
    )(sinks, sample_tile, z, z, z, z, z, z, c, sp, sm, qn.reshape(1, -1), kn.reshape(1, -1),
      glub.reshape(1, -1), cw, cb.reshape(1, -1), lng.reshape(1, -1), lnb.reshape(1, -1))


def _even_sample_body(sinks_ref, z_ref, ck_ref, cv_ref, sc_ref, c_ref, sp_ref, sm_ref, qn_ref, kn_ref,
                      glub_ref, cw_ref, cb_ref, lng_ref, lnb_ref,
                      mix_ref, nk_ref, nv_ref, nc_ref):
    zrow = lambda lo, hi: z_ref[0, :, lo:hi]
    c, sp, sm = c_ref[...], sp_ref[...], sm_ref[...]
    qn, kn = qn_ref[...], kn_ref[...]
    nbuf = WINDOW
    scale = HEAD_DIM ** -0.5
    qrow = lax.broadcasted_iota(jnp.int32, (V7X_SUBLANES, 1), 0)
    krow = lax.broadcasted_iota(jnp.int32, (nbuf, HEAD_DIM), 0)
    for g in range(N_KV_HEADS):
        ks = slice(g * HEAD_DIM, (g + 1) * HEAD_DIM)
        k_new = _norm_rope(zrow(Q_COLS + g * HEAD_DIM, Q_COLS + (g + 1) * HEAD_DIM), kn, c, sp, sm)
        v_new = zrow(Q_COLS + KV_COLS + g * HEAD_DIM, Q_COLS + KV_COLS + (g + 1) * HEAD_DIM)
        k_old = ck_ref[0, :, ks]
        v_old = cv_ref[0, :, ks]
        nk_ref[0, :, ks] = jnp.where(krow == nbuf - 1, k_new, pltpu.roll(k_old, nbuf - 1, 0))
        nv_ref[0, :, ks] = jnp.where(krow == nbuf - 1, v_new, pltpu.roll(v_old, nbuf - 1, 0))
        qs = [_norm_rope(zrow((g * Q_PER_KV + r) * HEAD_DIM, (g * Q_PER_KV + r + 1) * HEAD_DIM),
                         qn, c, sp, sm) for r in range(Q_PER_KV)]
        q8 = jnp.zeros((V7X_SUBLANES, HEAD_DIM), F32)
        for r in range(Q_PER_KV):
            q8 = jnp.where(qrow == r, qs[r], q8)
        q8b = q8.astype(BF16)
        s = lax.dot_general(q8b, k_old.astype(BF16), (((1,), (1,)), ((), ())),
                            preferred_element_type=F32) * scale
        k_new_r = k_new.astype(BF16).astype(F32)
        v_new_r = v_new.astype(BF16).astype(F32)
        s_self = jnp.sum(q8b.astype(F32) * k_new_r, axis=-1, keepdims=True) * scale
        sink = jnp.zeros((V7X_SUBLANES, 1), F32)
        for r in range(Q_PER_KV):
            sink = jnp.where(qrow == r, sinks_ref[g * Q_PER_KV + r], sink)
        m = jnp.maximum(jnp.maximum(jnp.max(s, axis=-1, keepdims=True), s_self), sink)
        p = jnp.exp(s - m)
        p_self = jnp.exp(s_self - m)
        denom = jnp.sum(p, axis=-1, keepdims=True) + p_self + jnp.exp(sink - m)
        o = jnp.dot((p / denom).astype(BF16), v_old.astype(BF16), preferred_element_type=F32)
        o = o + (p_self / denom).astype(BF16).astype(F32) * v_new_r
        for r in range(Q_PER_KV):
            h = g * Q_PER_KV + r
            mix_ref[0, :, h * HEAD_DIM:(h + 1) * HEAD_DIM] = o[r:r + 1, :]

    conv0 = Q_COLS + 2 * KV_COLS
    a = zrow(conv0, conv0 + CONV_CH) + glub_ref[:, 0:CONV_CH]
    gate = zrow(conv0 + CONV_CH, EVEN_IN) + glub_ref[:, CONV_CH:2 * CONV_CH]
    gl = a * _sigmoid(gate)
    nst = CONV_WIDTH - 1
    conv = jnp.sum(cw_ref[0:nst, :] * sc_ref[0], axis=0, keepdims=True) + cw_ref[nst:nst + 1, :] * gl
    y = _layer_norm_silu(conv + cb_ref[...], lng_ref[...], lnb_ref[...])
    mix_ref[0, :, ATT_WIDTH:ATT_WIDTH + CONV_CH] = y
    nc_ref[0, 0:nst - 1, :] = sc_ref[0, 1:nst, :]
    nc_ref[0, nst - 1:nst, :] = gl


def _even_core_sample(z, cache_k, cache_v, state_conv, sinks, qn, kn, glub, cw, cb, lng, lnb):
    nb = DEC_BATCH
    c, sp, sm = _rope_tables(jnp.full((1,), PAST_LEN, jnp.int32))
    vec = lambda n: pl.BlockSpec((1, n), lambda b: (0, 0))
    per_b = lambda r, n: pl.BlockSpec((1, r, n), lambda b: (b, 0, 0))
    nst = CONV_WIDTH - 1
    est = 2 * (EVEN_IN * 4 * nb + 4 * WINDOW * KV_COLS * 4 + 2 * 32 * CONV_CH * 4 + 4096 * 4 * 8) + (8 << 20)
    return pl.pallas_call(
        _even_sample_body,
        out_shape=(jax.ShapeDtypeStruct((nb, 1, 2 * ATT_WIDTH), F32),
                   jax.ShapeDtypeStruct((nb, WINDOW, KV_COLS), F32),
                   jax.ShapeDtypeStruct((nb, WINDOW, KV_COLS), F32),
                   jax.ShapeDtypeStruct((nb, nst, CONV_CH), F32)),
        grid=(nb,),
        in_specs=[pl.BlockSpec(memory_space=pltpu.SMEM),
                  per_b(1, EVEN_IN),
                  per_b(WINDOW, KV_COLS), per_b(WINDOW, KV_COLS), per_b(nst, CONV_CH),
                  vec(HEAD_DIM), vec(HEAD_DIM), vec(HEAD_DIM), vec(HEAD_DIM), vec(HEAD_DIM),
                  vec(2 * CONV_CH), pl.BlockSpec((CONV_WIDTH, CONV_CH), lambda b: (0, 0)),
                  vec(CONV_CH), vec(CONV_CH), vec(CONV_CH)],
        out_specs=(per_b(1, 2 * ATT_WIDTH), per_b(WINDOW, KV_COLS), per_b(WINDOW, KV_COLS),
                   per_b(nst, CONV_CH)),
        compiler_params=_cparams(1, est),
        name="even_core_sample",
    )(sinks, z[PROMPT_ROWS:PROMPT_ROWS + nb].reshape(nb, 1, EVEN_IN), cache_k, cache_v, state_conv, c, sp, sm,
      qn.reshape(1, -1), kn.reshape(1, -1), glub.reshape(1, -1), cw, cb.reshape(1, -1),
      lng.reshape(1, -1), lnb.reshape(1, -1))


def _sample_tile(sample_rows):
    pad = jnp.zeros((MERGE_TILE - DEC_BATCH, sample_rows.shape[1]), BF16)
    return jnp.concatenate([sample_rows.astype(BF16), pad], axis=0)


def _ssm_prep_body(lre_ref, lim_ref, ldt_ref, btre_ref, btim_ref,
                   are_ref, aim_ref, bbre_ref, bbim_ref, abre_ref, abim_ref, cre_ref, cim_ref):
    lre, lim = lre_ref[...], lim_ref[...]
    dt = jnp.exp(ldt_ref[...])
    mag = jnp.exp(lre * dt)
    ang = lim * dt
    are = mag * jnp.cos(ang)
    aim = mag * jnp.sin(ang)
    are_ref[...] = are
    aim_ref[...] = aim
    nre, nim = are - 1.0, aim
    den = lre * lre + lim * lim
    cre_ref[...] = (nre * lre + nim * lim) / den
    cim_ref[...] = (nim * lre - nre * lim) / den

    def per_group(g, carry):
        cr = cre_ref[pl.ds(g, 1), :]
        ci = cim_ref[pl.ds(g, 1), :]
        br, bi = btre_ref[g], btim_ref[g]
        bbr = cr * br - ci * bi
        bbi = cr * bi + ci * br
        bbre_ref[g] = bbr
        bbim_ref[g] = bbi
        ar = are_ref[pl.ds(g, 1), :]
        ai = aim_ref[pl.ds(g, 1), :]
        abre_ref[g] = ar * bbr - ai * bbi
        abim_ref[g] = ar * bbi + ai * bbr
        return carry

    lax.fori_loop(0, SSM_GROUPS, per_group, 0)


def _ssm_layout_body(bbre_ref, bbim_ref, abre_ref, abim_ref, cre_ref, cim_ref, b_ref, c_ref):
    n, ch, st = SSM_STATE, SSM_TILE_CH, SSM_TILE_ST
    p_shift = SSM_GROUP.bit_length() - 1
    n_shift = SSM_STATE.bit_length() - 1
    iota = lambda shape, d: lax.broadcasted_iota(jnp.int32, shape, d)
    spread = jnp.where((iota((n, st), 1) & (n - 1)) == iota((n, st), 0), 1.0, 0.0).astype(BF16)
    own_b = (iota((ch, st), 0) >> p_shift) == (iota((ch, st), 1) >> n_shift)
    for k, src_ref in enumerate((bbre_ref, bbim_ref, abre_ref, abim_ref)):
        rep = jnp.dot(src_ref[...].astype(BF16), spread, preferred_element_type=F32)
        r0, c0 = (k // 2) * ch, (k % 2) * st
        b_ref[0, r0:r0 + ch, c0:c0 + st] = jnp.where(own_b, rep, 0.0).astype(BF16)
    spread_t = jnp.where((iota((st, n), 0) & (n - 1)) == iota((st, n), 1), 1.0, 0.0).astype(BF16)
    own_c = (iota((st, ch), 0) >> n_shift) == (iota((st, ch), 1) >> p_shift)
    for k, (src_ref, sign) in enumerate(((cre_ref, 1.0), (cim_ref, -1.0))):
        rep = lax.dot_general(spread_t, src_ref[...].astype(BF16), (((1,), (1,)), ((), ())),
                              preferred_element_type=F32)
        c_ref[0, k * st:(k + 1) * st, :] = jnp.where(own_c, sign * rep, 0.0).astype(BF16)


def _ssm_params(lam_re, lam_im, log_dt, b_re, b_im, c_re, c_im, d_skip):
    g, n, p = SSM_GROUPS, SSM_STATE, SSM_GROUP
    bt_re = jnp.swapaxes(b_re, 1, 2)
    bt_im = jnp.swapaxes(b_im, 1, 2)
    gn = jax.ShapeDtypeStruct((g, n), F32)
    gpn = jax.ShapeDtypeStruct((g, p, n), F32)
    are, aim, bbre, bbim, abre, abim = pl.pallas_call(
        _ssm_prep_body,
        out_shape=(gn, gn, gpn, gpn, gpn, gpn),
        scratch_shapes=[pltpu.VMEM((g, n), F32), pltpu.VMEM((g, n), F32)],
        name="ssm_prep",
    )(lam_re, lam_im, log_dt.reshape(g, 1), bt_re, bt_im)

    t, tg = SSM_TILES, SSM_TILE_GROUPS
    ch, st = SSM_TILE_CH, SSM_TILE_ST
    rows_gp = lambda m: m.reshape(g * p, n)
    src_spec = pl.BlockSpec((ch, n), lambda i: (i, 0))
    b_mat, c_mat = pl.pallas_call(
        _ssm_layout_body,
        out_shape=(jax.ShapeDtypeStruct((t, 2 * ch, 2 * st), BF16),
                   jax.ShapeDtypeStruct((t, 2 * st, ch), BF16)),
        grid=(t,),
        in_specs=[src_spec] * 6,
        out_specs=(pl.BlockSpec((1, 2 * ch, 2 * st), lambda i: (i, 0, 0)),
                   pl.BlockSpec((1, 2 * st, ch), lambda i: (i, 0, 0))),
        compiler_params=_cparams(1, 2 * (2 * ch * 2 * st * 2 + 2 * st * ch * 2) + 8 * ch * st * 4),
        name="ssm_layout",
    )(rows_gp(bbre), rows_gp(bbim), rows_gp(abre), rows_gp(abim), rows_gp(c_re), rows_gp(c_im))
    return (b_mat, c_mat, are.reshape(t, 1, tg * n), aim.reshape(t, 1, tg * n),
            d_skip.reshape(t, 1, tg * p))


def _gelu_tanh(y):
    return 0.5 * y * (1.0 + jnp.tanh(math.sqrt(2.0 / math.pi) * (y + 0.044715 * (y * y * y))))


def _ssm_prompt_body(u0_ref, u1_ref, u2_ref, u3_ref, tile_ref, b_ref, c_ref, are_ref, aim_ref, d_ref,
                     y_ref, hre_ref, him_ref,
                     us_ref, ys_ref, bu_ref, h_ref, carry_ref, p_re_ref, p_im_ref):
    st = SSM_TILE_ST
    sub = V7X_SUBLANES
    nl = st // V7X_LANES
    nh = SSM_TILE_CH // V7X_LANES
    tt = pl.program_id(1)
    steps = u0_ref.shape[0]
    rows = steps * BATCH

    @pl.when(tt == 0)
    def _init():
        carry_ref[...] = jnp.zeros_like(carry_ref)
        ar = jnp.broadcast_to(are_ref[0], (sub, st))
        ai = jnp.broadcast_to(aim_ref[0], (sub, st))
        lo = lax.broadcasted_iota(jnp.int32, (sub, st), 0) < BATCH
        p_re_ref[...] = jnp.where(lo, ar, ar * ar - ai * ai)
        p_im_ref[...] = jnp.where(lo, ai, 2.0 * ar * ai)

    for b, ub_ref in enumerate((u0_ref, u1_ref, u2_ref, u3_ref)):
        for hh in range(nh):
            us_ref[hh, pl.ds(b, steps, stride=BATCH), :] = ub_ref[:, hh * V7X_LANES:(hh + 1) * V7X_LANES]
    u = jnp.concatenate([us_ref[hh] for hh in range(nh)], axis=1)
    second = (lax.broadcasted_iota(jnp.int32, u.shape, 0) % sub) >= BATCH
    u_prev = jnp.where(second, pltpu.roll(u, BATCH, 0), 0.0)
    u2 = jnp.concatenate([u, u_prev], axis=1).astype(BF16)
    half = rows // SSM_PARTS
    for hf in range(SSM_PARTS):
        bu_ref[hf] = jnp.dot(u2[hf * half:(hf + 1) * half], b_ref[0], preferred_element_type=F32)

    lo8 = lax.broadcasted_iota(jnp.int32, (sub, V7X_LANES), 0) < BATCH
    pack = 2 * sub

    def scan_half(hf, carry):
        for i in range(half // pack):
            r0 = i * pack
            new = []
            for j in range(nl):
                lre = slice(j * V7X_LANES, (j + 1) * V7X_LANES)
                lim = slice(st + j * V7X_LANES, st + (j + 1) * V7X_LANES)
                pr, pi = p_re_ref[:, lre], p_im_ref[:, lre]
                hr, hi = carry[2 * j], carry[2 * j + 1]
                outs_r, outs_i = [], []
                for k in range(2):
                    yr = bu_ref[hf, r0 + k * sub:r0 + (k + 1) * sub, lre]
                    yi = bu_ref[hf, r0 + k * sub:r0 + (k + 1) * sub, lim]
                    hbr = jnp.where(lo8, pltpu.roll(hr, BATCH, 0), hr)
                    hbi = jnp.where(lo8, pltpu.roll(hi, BATCH, 0), hi)
                    hr = yr + pr * hbr - pi * hbi
                    hi = yi + pr * hbi + pi * hbr
                    outs_r.append(hr)
                    outs_i.append(hi)
                h_ref[hf, r0:r0 + pack, lre] = jnp.concatenate(outs_r, axis=0).astype(h_ref.dtype)
                h_ref[hf, r0:r0 + pack, lim] = jnp.concatenate(outs_i, axis=0).astype(h_ref.dtype)
                new += [hr, hi]
            carry = tuple(new)
        return carry

    carry = []
    for j in range(nl):
        carry += [carry_ref[:, j * V7X_LANES:(j + 1) * V7X_LANES],
                  carry_ref[:, st + j * V7X_LANES:st + (j + 1) * V7X_LANES]]
    carry = tuple(carry)
    ys = []
    for hf in range(SSM_PARTS):
        carry = scan_half(hf, carry)
        ys.append(jnp.dot(h_ref[hf], c_ref[0], preferred_element_type=F32))
    for j in range(nl):
        carry_ref[:, j * V7X_LANES:(j + 1) * V7X_LANES] = carry[2 * j]
        carry_ref[:, st + j * V7X_LANES:st + (j + 1) * V7X_LANES] = carry[2 * j + 1]

    y = jnp.concatenate(ys, axis=0) + d_ref[0] * u
    y = _gelu_tanh(y)
    for hh in range(nh):
        ys_ref[hh] = y[:, hh * V7X_LANES:(hh + 1) * V7X_LANES]
    for b in range(BATCH):
        for hh in range(nh):
            y_ref[b, :, hh * V7X_LANES:(hh + 1) * V7X_LANES] = (
                ys_ref[hh, pl.ds(b, steps, stride=BATCH), :].astype(y_ref.dtype))
    y_ref[BATCH] = jnp.zeros((steps, SSM_TILE_CH), y_ref.dtype)

    @pl.when(tt == 0)
    def _sample_tile():
        y_ref[BATCH, 0:MERGE_TILE, :] = tile_ref[...]

    hre_ref[...] = carry_ref[:, 0:st]
    him_ref[...] = carry_ref[:, st:2 * st]


def _ssm_prompt(u, sample_tile, b_mat, c_mat, are, aim, d):
    steps = SSM_STEPS
    rows = steps * BATCH
    nt = SEQ // steps
    st, ch = SSM_TILE_ST, SSM_TILE_CH
    nh = ch // V7X_LANES
    est = (8 * steps * ch * 4 + 2 * BATCH * steps * ch * 2 + 6 * ch * 2 * st * 2 + 2 * rows * 2 * st * 4
           + rows * 2 * st * 2 + 4 * rows * ch * 4 + 8 * 8 * st * 4 + (4 << 20))
    u_spec = lambda b: pl.BlockSpec((steps, ch), lambda g, t, b=b: (b * nt + t, g))
    return pl.pallas_call(
        _ssm_prompt_body,
        out_shape=(jax.ShapeDtypeStruct((BATCH + 1, SEQ, D_MODEL), BF16),
                   jax.ShapeDtypeStruct((V7X_SUBLANES, SSM_GROUPS * SSM_STATE), F32),
                   jax.ShapeDtypeStruct((V7X_SUBLANES, SSM_GROUPS * SSM_STATE), F32)),
        grid=(SSM_TILES, nt),
        in_specs=[u_spec(0), u_spec(1), u_spec(2), u_spec(3),
                  pl.BlockSpec((MERGE_TILE, ch), lambda g, t: (0, g)),
                  pl.BlockSpec((1, 2 * ch, 2 * st), lambda g, t: (g, 0, 0)),
                  pl.BlockSpec((1, 2 * st, ch), lambda g, t: (g, 0, 0)),
                  pl.BlockSpec((1, 1, st), lambda g, t: (g, 0, 0)),
                  pl.BlockSpec((1, 1, st), lambda g, t: (g, 0, 0)),
                  pl.BlockSpec((1, 1, ch), lambda g, t: (g, 0, 0))],
        out_specs=(pl.BlockSpec((BATCH + 1, steps, ch), lambda g, t: (0, t, g)),
                   pl.BlockSpec((V7X_SUBLANES, st), lambda g, t: (0, g)),
                   pl.BlockSpec((V7X_SUBLANES, st), lambda g, t: (0, g))),
        scratch_shapes=[pltpu.VMEM((nh, rows, V7X_LANES), F32), pltpu.VMEM((nh, rows, V7X_LANES), F32),
                        pltpu.VMEM((SSM_PARTS, rows // SSM_PARTS, 2 * st), F32),
                        pltpu.VMEM((SSM_PARTS, rows // SSM_PARTS, 2 * st), BF16),
                        pltpu.VMEM((V7X_SUBLANES, 2 * st), F32),
                        pltpu.VMEM((V7X_SUBLANES, st), F32), pltpu.VMEM((V7X_SUBLANES, st), F32)],
        compiler_params=_cparams(2, est),
        name="ssm_prompt",
    )(u, u, u, u, sample_tile, b_mat, c_mat, are, aim, d)


def _ssm_sample_body(u_ref, h0re_ref, h0im_ref, b_ref, c_ref, are_ref, aim_ref, d_ref,
                     y_ref, hre_ref, him_ref):
    st = SSM_TILE_ST
    u = u_ref[...]
    bu = jnp.dot(u.astype(BF16), b_ref[0], preferred_element_type=F32)
    ar, ai = are_ref[0], aim_ref[0]
    h0r, h0i = h0re_ref[...], h0im_ref[...]
    hr = ar * h0r - ai * h0i + bu[:, 0:st]
    hi = ar * h0i + ai * h0r + bu[:, st:2 * st]
    hre_ref[...] = hr
    him_ref[...] = hi
    hcat = jnp.concatenate([hr, hi], axis=1).astype(BF16)
    y = jnp.dot(hcat, c_ref[0], preferred_element_type=F32) + d_ref[0] * u
    y_ref[...] = _gelu_tanh(y).astype(y_ref.dtype)


def _ssm_sample(u, h0_re, h0_im, b_mat, c_mat, are, aim, d):
    nb = DEC_BATCH
    st, ch = SSM_TILE_ST, SSM_TILE_CH
    est = 2 * (nb * ch * 6 + 4 * nb * st * 4 + 2 * ch * 2 * st * 2) + (4 << 20)
    return pl.pallas_call(
        _ssm_sample_body,
        out_shape=(jax.ShapeDtypeStruct((nb, D_MODEL), BF16),
                   jax.ShapeDtypeStruct((nb, SSM_GROUPS * SSM_STATE), F32),
                   jax.ShapeDtypeStruct((nb, SSM_GROUPS * SSM_STATE), F32)),
        grid=(SSM_TILES,),
        in_specs=[pl.BlockSpec((nb, ch), lambda g: (PROMPT_ROWS // DEC_BATCH, g)),
                  pl.BlockSpec((nb, st), lambda g: (0, g)),
                  pl.BlockSpec((nb, st), lambda g: (0, g)),
                  pl.BlockSpec((1, ch, 2 * st), lambda g: (g, 0, 0)),
                  pl.BlockSpec((1, 2 * st, ch), lambda g: (g, 0, 0)),
                  pl.BlockSpec((1, 1, st), lambda g: (g, 0, 0)),
                  pl.BlockSpec((1, 1, st), lambda g: (g, 0, 0)),
                  pl.BlockSpec((1, 1, ch), lambda g: (g, 0, 0))],
        out_specs=(pl.BlockSpec((nb, ch), lambda g: (0, g)),
                   pl.BlockSpec((nb, st), lambda g: (0, g)),
                   pl.BlockSpec((nb, st), lambda g: (0, g))),
        compiler_params=_cparams(1, est),
        name="ssm_sample",
    )(u, h0_re, h0_im, b_mat, c_mat, are, aim, d)


def kernel(x_prompt, x_sample, cache_swa_k, cache_swa_v, state_conv, state_ssm_re, state_ssm_im, ffn_norm, ffn_w_gu, ffn_w_down, mix_norm, even_w_in, even_q_norm, even_k_norm, even_sinks, even_glu_b, even_conv_w, even_conv_b, even_ln_g, even_ln_b, even_w_out, odd_w_in, odd_lam_re, odd_lam_im, odd_log_dt, odd_b_re, odd_b_im, odd_c_re, odd_c_im, odd_d, odd_w_gate, odd_w_out):
    nb = DEC_BATCH
    ssm = _ssm_params(odd_lam_re[0], odd_lam_im[0], odd_log_dt[0], odd_b_re[0], odd_b_im[0],
                      odd_c_re[0], odd_c_im[0], odd_d[0])

    x, h = _merge_rmsnorm(x_prompt.reshape(PROMPT_ROWS, D_MODEL), x_sample.reshape(nb, D_MODEL),
                          ffn_norm[0, 0])
    x, gu_next = _half_ffn(x, h, ffn_w_gu, ffn_w_down, (0, 0), next_gu=(0, 1))

    even_p = (even_sinks[0], even_q_norm[0], even_k_norm[0], even_glu_b[0], even_conv_w[0],
              even_conv_b[0], even_ln_g[0], even_ln_b[0])
    z = _matmul(_rmsnorm(x, mix_norm[0]), even_w_in, (0,), n_cols=EVEN_IN, name="even_in")
    mix_s, sk, sv, sc = _even_core_sample(
        z, cache_swa_k.reshape(nb, WINDOW, KV_COLS), cache_swa_v.reshape(nb, WINDOW, KV_COLS),
        state_conv.reshape(nb, CONV_WIDTH - 1, CONV_CH), *even_p)
    mix, pk, pv, pc = _even_core_prompt(z, _sample_tile(mix_s.reshape(nb, D_MODEL)), *even_p)
    x = _matmul(mix, even_w_out, (0,), n_cols=D_MODEL, epilogue="res", extra=x, name="even_out")
    x, gu_next = _half_ffn(x, _rmsnorm(x, ffn_norm[0, 1]), ffn_w_gu, ffn_w_down, (0, 1),
                           gu_bf16=gu_next, next_gu=(1, 0))

    x, gu_next = _half_ffn(x, _rmsnorm(x, ffn_norm[1, 0]), ffn_w_gu, ffn_w_down, (1, 0),
                           gu_bf16=gu_next, next_gu=(1, 1))
    u = _matmul(_rmsnorm(x, mix_norm[1]), odd_w_in, (0,), n_cols=D_MODEL, name="odd_in")
    y_s, sre, sim = _ssm_sample(u, state_ssm_re.reshape(nb, -1), state_ssm_im.reshape(nb, -1), *ssm)
    y3, pre8, pim8 = _ssm_prompt(u, _sample_tile(y_s), *ssm)
    y = y3.reshape((BATCH + 1) * SEQ, D_MODEL)
    yg = _matmul(y, odd_w_gate, (0,), n_cols=D_MODEL, epilogue="gate", out_dtype=BF16, extra=y,
                 name="odd_gate")
    x = _matmul(yg, odd_w_out, (0,), n_cols=D_MODEL, epilogue="res", extra=x, name="odd_out")
    (y_p, y_smp), _ = _half_ffn(x, _rmsnorm(x, ffn_norm[1, 1]), ffn_w_gu, ffn_w_down, (1, 1),
                                gu_bf16=gu_next, split_out=True)

    kv5 = lambda t, b: t.reshape(1, b, WINDOW, N_KV_HEADS, HEAD_DIM)
    st4 = lambda t, b: t.reshape(1, b, SSM_GROUPS, SSM_STATE)
    return (y_p.reshape(BATCH, SEQ, D_MODEL), y_smp.reshape(nb, 1, D_MODEL),
            kv5(pk, BATCH), kv5(pv, BATCH), pc.reshape(1, BATCH, CONV_WIDTH - 1, CONV_CH),
            st4(pre8[BATCH:2 * BATCH], BATCH), st4(pim8[BATCH:2 * BATCH], BATCH),
            kv5(sk, nb), kv5(sv, nb), sc.reshape(1, nb, CONV_WIDTH - 1, CONV_CH),
            st4(sre, nb), st4(sim, nb))
```

```python
import functools
import math

import jax
import jax.numpy as jnp
from jax import lax
from jax.experimental import pallas as pl
from jax.experimental.pallas import tpu as pltpu

F32 = jnp.float32
BF16 = jnp.bfloat16

D_MODEL = 4096
BATCH = 4
SEQ = 2048
DEC_BATCH = 32
PAST_LEN = 16384
HEAD_DIM = 128
N_HEADS = 16
N_KV_HEADS = 4
Q_PER_KV = 4
WINDOW = 128
ROT_DIM = 32
ROPE_THETA = 500000.0
ATT_WIDTH = 2048
CONV_CH = 2048
CONV_WIDTH = 31
Q_COLS = 2048
KV_COLS = 512
EVEN_IN = 7168
SSM_GROUPS = 256
SSM_GROUP = 16
SSM_STATE = 64
D_FF = 11008
EPS = 1e-6
NEG_INF = -1e30

V7X_VMEM_BYTES = 64 * 1024 * 1024
V7X_LANES = 128
V7X_SUBLANES = 8
VMEM_CAP = V7X_VMEM_BYTES - 3 * 1024 * 1024

PROMPT_ROWS = BATCH * SEQ
MERGE_TILE = 128
M_ROWS = PROMPT_ROWS + MERGE_TILE
ROW_TILES = 8
TM = M_ROWS // ROW_TILES
TM_DOWN = TM // 2

SSM_TILE_GROUPS = 16
SSM_TILE_CH = SSM_TILE_GROUPS * SSM_GROUP
SSM_TILE_ST = SSM_TILE_GROUPS * SSM_STATE
SSM_TILES = SSM_GROUPS // SSM_TILE_GROUPS
SSM_STEPS = 512
SSM_PARTS = 8
CONV_HALO = 32


def _cparams(n_grid, est_bytes):
    limit = int(min(VMEM_CAP, est_bytes * 5 // 4 + (4 << 20)))
    return pltpu.CompilerParams(dimension_semantics=("arbitrary",) * n_grid,
                                vmem_limit_bytes=limit)


def _sigmoid(x):
    return 0.5 * (1.0 + jnp.tanh(0.5 * x))


def _rms(x, g):
    ms = jnp.mean(x * x, axis=-1, keepdims=True)
    return x * lax.rsqrt(ms + EPS) * g


def _rmsnorm_body(x_ref, g_ref, o_ref):
    o_ref[...] = _rms(x_ref[...], g_ref[...]).astype(o_ref.dtype)


def _rmsnorm(x, g):
    d = x.shape[1]
    tr = 2 * TM // 5
    return pl.pallas_call(
        _rmsnorm_body,
        out_shape=jax.ShapeDtypeStruct((M_ROWS, d), BF16),
        grid=(M_ROWS // tr,),
        in_specs=[pl.BlockSpec((tr, d), lambda i: (i, 0)),
                  pl.BlockSpec((1, d), lambda i: (0, 0))],
        out_specs=pl.BlockSpec((tr, d), lambda i: (i, 0)),
        compiler_params=_cparams(1, 2 * tr * d * 4 + 2 * tr * d * 2),
        name="rmsnorm",
    )(x, g.reshape(1, d))


def _merge_rmsnorm_body(xp_ref, xs_ref, g_ref, x_ref, h_ref):
    i = pl.program_id(0)
    g = g_ref[...]

    @pl.when(i < PROMPT_ROWS // MERGE_TILE)
    def _prompt():
        x = xp_ref[...]
        x_ref[...] = x
        h_ref[...] = _rms(x, g).astype(h_ref.dtype)

    @pl.when(i == PROMPT_ROWS // MERGE_TILE)
    def _sample():
        xs = xs_ref[...]
        pad = MERGE_TILE - DEC_BATCH
        x_ref[0:DEC_BATCH, :] = xs
        x_ref[DEC_BATCH:MERGE_TILE, :] = jnp.zeros((pad, D_MODEL), F32)
        h_ref[0:DEC_BATCH, :] = _rms(xs, g).astype(h_ref.dtype)
        h_ref[DEC_BATCH:MERGE_TILE, :] = jnp.zeros((pad, D_MODEL), h_ref.dtype)


def _merge_rmsnorm(xp, xs, g):
    d = D_MODEL
    last = PROMPT_ROWS // MERGE_TILE - 1
    return pl.pallas_call(
        _merge_rmsnorm_body,
        out_shape=(jax.ShapeDtypeStruct((M_ROWS, d), F32), jax.ShapeDtypeStruct((M_ROWS, d), BF16)),
        grid=(M_ROWS // MERGE_TILE,),
        in_specs=[pl.BlockSpec((MERGE_TILE, d), lambda i: (jnp.minimum(i, last), 0)),
                  pl.BlockSpec((DEC_BATCH, d), lambda i: (0, 0)),
                  pl.BlockSpec((1, d), lambda i: (0, 0))],
        out_specs=(pl.BlockSpec((MERGE_TILE, d), lambda i: (i, 0)),
                   pl.BlockSpec((MERGE_TILE, d), lambda i: (i, 0))),
        compiler_params=_cparams(1, 4 * MERGE_TILE * d * 4 + 2 * MERGE_TILE * d * 2 + 2 * DEC_BATCH * d * 4),
        name="merge_rmsnorm",
    )(xp, xs, g.reshape(1, d))


def _mm_body(*refs, n_w, cast_w, epilogue, scale, side_cast, split_rows):
    a_ref = refs[0]
    w_refs = refs[1:1 + n_w]
    pos = 1 + n_w
    extra_ref = side_in_ref = side_out_ref = o2_ref = None
    if epilogue in ("res", "gate"):
        extra_ref = refs[pos]
        pos += 1
    if side_cast:
        side_in_ref = refs[pos]
        pos += 1
    o_ref = refs[pos]
    pos += 1
    if side_cast:
        side_out_ref = refs[pos]
        pos += 1
    if split_rows is not None:
        o2_ref = refs[pos]
        pos += 1
    tn = w_refs[0].shape[-1]
    if cast_w:
        wb_ref = refs[pos]

        @pl.when(pl.program_id(1) == 0)
        def _cast_weights():
            for n, w_ref in enumerate(w_refs):
                wb_ref[:, n * tn:(n + 1) * tn] = w_ref[...].astype(BF16)
    if side_cast:
        side_out_ref[...] = side_in_ref[...].astype(side_out_ref.dtype)

    a = a_ref[...].astype(BF16)
    if cast_w:
        acc = jnp.dot(a, wb_ref[...], preferred_element_type=F32)
        accs = [acc[:, n * tn:(n + 1) * tn] for n in range(n_w)]
    else:
        accs = [jnp.dot(a, w_ref[...], preferred_element_type=F32) for w_ref in w_refs]
    acc = accs[0]
    if epilogue == "plain":
        out = acc
    elif epilogue == "swiglu":
        gate, up = accs
        out = gate * _sigmoid(gate) * up
    elif epilogue == "res":
        out = extra_ref[...] + scale * acc
    elif epilogue == "gate":
        out = extra_ref[...].astype(F32) * _sigmoid(acc)
    o_ref[...] = out.astype(o_ref.dtype)

    if split_rows is not None:
        @pl.when(pl.program_id(1) == pl.num_programs(1) - 1)
        def _sample_rows():
            o2_ref[...] = out[split_rows:split_rows + DEC_BATCH, :].astype(o2_ref.dtype)


def _matmul(a, w, widx=(), *, n_cols, tm=TM, tn=512, epilogue="plain", out_dtype=F32, col_offs=(0,),
            extra=None, scale=1.0, side=None, split_out=False, name="mm"):
    k = a.shape[1]
    n_w = len(col_offs)
    cast_w = w.dtype != BF16
    nlead = len(widx)
    n_i = M_ROWS // tm
    n_j = n_cols // tn

    operands = [a]
    in_specs = [pl.BlockSpec((tm, k), lambda j, i: (i, 0))]
    for off in col_offs:
        operands.append(w)
        in_specs.append(pl.BlockSpec((None,) * nlead + (k, tn),
                                     lambda j, i, off=off: tuple(widx) + (0, j + off)))
    w_bytes = w.dtype.itemsize
    est = 2 * tm * k * a.dtype.itemsize + n_w * (2 * k * tn * w_bytes + (k * tn * 2 if cast_w else 0))
    if extra is not None:
        operands.append(extra)
        in_specs.append(pl.BlockSpec((tm, tn), lambda j, i: (i, j)))
        est += 2 * tm * tn * extra.dtype.itemsize
    out_shapes, out_specs = [], []
    if split_out:
        out_shapes.append(jax.ShapeDtypeStruct((PROMPT_ROWS, n_cols), out_dtype))
    else:
        out_shapes.append(jax.ShapeDtypeStruct((M_ROWS, n_cols), out_dtype))
    out_specs.append(pl.BlockSpec((tm, tn), lambda j, i: (i, j)))
    if side is not None:
        w_src, widx_src, rows_total = side
        slab = rows_total // (n_i * n_j)
        assert slab * n_i * n_j == rows_total and slab % 16 == 0
        cols_src = w_src.shape[-1]
        operands.append(w_src)
        in_specs.append(pl.BlockSpec((None,) * len(widx_src) + (slab, cols_src),
                                     lambda j, i: tuple(widx_src) + (j * n_i + i, 0)))
        out_shapes.append(jax.ShapeDtypeStruct((rows_total, cols_src), BF16))
        out_specs.append(pl.BlockSpec((slab, cols_src), lambda j, i: (j * n_i + i, 0)))
        est += 2 * slab * cols_src * 6
    split_rows = None
    if split_out:
        split_rows = PROMPT_ROWS - (n_i - 1) * tm
        assert split_rows % V7X_SUBLANES == 0 and 0 <= split_rows <= tm - DEC_BATCH
        out_shapes.append(jax.ShapeDtypeStruct((DEC_BATCH, n_cols), out_dtype))
        out_specs.append(pl.BlockSpec((DEC_BATCH, tn), lambda j, i: (0, j)))
    est += 2 * tm * tn * jnp.dtype(out_dtype).itemsize + (n_w + 1) * tm * tn * 4

    outs = pl.pallas_call(
        functools.partial(_mm_body, n_w=n_w, cast_w=cast_w, epilogue=epilogue, scale=scale,
                          side_cast=side is not None, split_rows=split_rows),
        out_shape=tuple(out_shapes),
        grid=(n_j, n_i),
        in_specs=in_specs,
        out_specs=tuple(out_specs),
        scratch_shapes=[pltpu.VMEM((k, n_w * tn), BF16)] if cast_w else [],
        compiler_params=_cparams(2, est),
        name=name,
    )(*operands)
    return outs[0] if len(outs) == 1 else outs


def _half_ffn(x, h, w_gu, w_down, widx, *, gu_bf16=None, next_gu=None, split_out=False):
    if gu_bf16 is None:
        gu_w, gu_idx, gu_tm = w_gu, widx, TM
    else:
        gu_w, gu_idx, gu_tm = gu_bf16, (), 2 * TM
    act, wd_bf16 = _matmul(h, gu_w, gu_idx, n_cols=D_FF, tm=gu_tm, tn=256, epilogue="swiglu",
                           out_dtype=BF16, col_offs=(0, D_FF // 256), side=(w_down, widx, D_FF),
                           name="ffn_gu")
    side = None if next_gu is None else (w_gu, next_gu, D_MODEL)
    outs = _matmul(act, wd_bf16, n_cols=D_MODEL, tm=TM_DOWN, tn=512, epilogue="res", extra=x, scale=0.5,
                   side=side, split_out=split_out, name="ffn_down")
    if next_gu is None:
        return outs, None
    return outs[0], outs[1]


def _rope_tables(pos):
    half = ROT_DIM // 2
    inv_freq = jnp.power(jnp.float32(ROPE_THETA), -jnp.arange(half, dtype=F32) * (2.0 / ROT_DIM))
    ang = pos.astype(F32)[:, None] * inv_freq[None, :]
    cos, sin = jnp.cos(ang), jnp.sin(ang)
    n = pos.shape[0]
    rest = HEAD_DIM - ROT_DIM
    c = jnp.concatenate([cos, cos, jnp.ones((n, rest), F32)], axis=1)
    sp = jnp.concatenate([jnp.zeros((n, half), F32), sin, jnp.zeros((n, rest), F32)], axis=1)
    sm = jnp.concatenate([-sin, jnp.zeros((n, HEAD_DIM - half), F32)], axis=1)
    return c, sp, sm


def _norm_rope(x, g, c, sp, sm):
    half = ROT_DIM // 2
    y = x * lax.rsqrt(jnp.mean(x * x, axis=-1, keepdims=True) + EPS) * g
    return y * c + pltpu.roll(y, half, 1) * sp + pltpu.roll(y, HEAD_DIM - half, 1) * sm


def _layer_norm_silu(c, g, b):
    mu = jnp.mean(c, axis=-1, keepdims=True)
    xc = c - mu
    var = jnp.mean(xc * xc, axis=-1, keepdims=True)
    y = xc * lax.rsqrt(var + EPS) * g + b
    return y * _sigmoid(y)


def _even_prompt_body(sinks_ref, tile_ref, *refs):
    nblk = BATCH * (SEQ // WINDOW)
    s = pl.program_id(0)

    @pl.when(s < nblk)
    def _prompt_block():
        _even_prompt_block(s % (SEQ // WINDOW), sinks_ref, *refs)

    @pl.when(s == nblk)
    def _sample_tile():
        mix_ref = refs[16]
        mix_ref[...] = tile_ref[...]


def _even_prompt_block(i, sinks_ref, q_ref, kv_ref, a0_ref, a1_ref, g0_ref, g1_ref,
                       c_ref, sp_ref, sm_ref, qn_ref, kn_ref, glub_ref, cw_ref, cb_ref,
                       lng_ref, lnb_ref,
                       mix_ref, nk_ref, nv_ref, nc_ref,
                       kprev_ref, vprev_ref, gbuf_ref, cbuf_ref,
                       q4_ref, k2_ref, v2_ref, obuf_ref, cwbuf_ref, zraw_ref, kf_ref):
    blk = WINDOW

    @pl.when(i == 0)
    def _reset():
        kprev_ref[...] = jnp.zeros_like(kprev_ref)
        vprev_ref[...] = jnp.zeros_like(vprev_ref)
        gbuf_ref[0:CONV_HALO, :] = jnp.zeros((CONV_HALO, CONV_CH), F32)
        cwbuf_ref[0:CONV_WIDTH, :] = cw_ref[...]

    c, sp, sm = c_ref[...], sp_ref[...], sm_ref[...]
    qn, kn = qn_ref[...], kn_ref[...]
    glub = glub_ref[...]

    half_ch = CONV_CH // 2
    conv0 = Q_COLS + 2 * KV_COLS
    zraw_ref[:, 0:Q_COLS] = q_ref[...]
    zraw_ref[:, Q_COLS:conv0] = kv_ref[...]
    for n, part_ref in enumerate((a0_ref, a1_ref, g0_ref, g1_ref)):
        zraw_ref[:, conv0 + n * half_ch:conv0 + (n + 1) * half_ch] = part_ref[...]

    a = zraw_ref[:, conv0:conv0 + CONV_CH] + glub[:, 0:CONV_CH]
    gate = zraw_ref[:, conv0 + CONV_CH:EVEN_IN] + glub[:, CONV_CH:2 * CONV_CH]
    gbuf_ref[CONV_HALO:CONV_HALO + blk, :] = a * _sigmoid(gate)

    lane_chunk = V7X_LANES
    first = CONV_HALO - (CONV_WIDTH - 1)
    sub = V7X_SUBLANES

    def conv_chunk(ci):
        ls = slice(ci * lane_chunk, (ci + 1) * lane_chunk)
        acc = jnp.zeros((blk, lane_chunk), F32)
        for s in range(sub):
            taps = [w for w in range(CONV_WIDTH) if (first + w) % sub == s]
            rows = blk if s == 0 else blk + sub
            part = jnp.zeros((rows, lane_chunk), F32)
            for w in taps:
                base = first + w - s
                part = part + cwbuf_ref[w:w + 1, ls] * gbuf_ref[base:base + rows, ls]
            acc = acc + part[s:s + blk, :]
        cbuf_ref[:, ls] = acc

    assert CONV_CH // lane_chunk == N_HEADS
    row = lax.broadcasted_iota(jnp.int32, (Q_PER_KV * blk, 2 * blk), 0) & (blk - 1)
    col = lax.broadcasted_iota(jnp.int32, (Q_PER_KV * blk, 2 * blk), 1)
    first_key = jnp.where(i > 0, 0, blk)
    mask4 = (col >= jnp.maximum(row, first_key)) & (col <= row + WINDOW)
    scale = HEAD_DIM ** -0.5
    for g in range(N_KV_HEADS):
        ks = slice(g * HEAD_DIM, (g + 1) * HEAD_DIM)
        k_cur = _norm_rope(zraw_ref[:, Q_COLS + g * HEAD_DIM:Q_COLS + (g + 1) * HEAD_DIM], kn, c, sp, sm)
        v_cur = zraw_ref[:, Q_COLS + KV_COLS + g * HEAD_DIM:Q_COLS + KV_COLS + (g + 1) * HEAD_DIM]
        kf_ref[:, ks] = k_cur
        k_cur_b = k_cur.astype(BF16)
        v_cur_b = v_cur.astype(BF16)
        k2_ref[g, 0:blk, :] = kprev_ref[:, ks]
        k2_ref[g, blk:2 * blk, :] = k_cur_b
        v2_ref[g, 0:blk, :] = vprev_ref[:, ks]
        v2_ref[g, blk:2 * blk, :] = v_cur_b
        kprev_ref[:, ks] = k_cur_b
        vprev_ref[:, ks] = v_cur_b
        for r in range(Q_PER_KV):
            h = g * Q_PER_KV + r
            q4_ref[g, r * blk:(r + 1) * blk, :] = _norm_rope(
                zraw_ref[:, h * HEAD_DIM:(h + 1) * HEAD_DIM], qn, c, sp, sm).astype(BF16)

    for g in range(N_KV_HEADS):
        heads = [g * Q_PER_KV + r for r in range(Q_PER_KV)]
        sink = jnp.concatenate([jnp.full((blk, 1), sinks_ref[h], F32) for h in heads], axis=0)
        s = lax.dot_general(q4_ref[g], k2_ref[g], (((1,), (1,)), ((), ())),
                            preferred_element_type=F32) * scale
        s = jnp.where(mask4, s, NEG_INF)
        m = jnp.maximum(jnp.max(s, axis=-1, keepdims=True), sink)
        p = jnp.exp(s - m)
        denom = jnp.sum(p, axis=-1, keepdims=True) + jnp.exp(sink - m)
        w = (p / denom).astype(BF16)
        o = jnp.dot(w, v2_ref[g], preferred_element_type=F32)
        for r, h in enumerate(heads):
            obuf_ref[:, h * HEAD_DIM:(h + 1) * HEAD_DIM] = o[r * blk:(r + 1) * blk, :].astype(obuf_ref.dtype)
            conv_chunk(h)

    nk_ref[0] = kf_ref[...]
    nv_ref[0] = zraw_ref[:, Q_COLS + KV_COLS:conv0]
    mix_ref[:, 0:ATT_WIDTH] = obuf_ref[...]
    y = _layer_norm_silu(cbuf_ref[...] + cb_ref[...], lng_ref[...], lnb_ref[...])
    mix_ref[:, ATT_WIDTH:ATT_WIDTH + CONV_CH] = y.astype(mix_ref.dtype)

    @pl.when(i == SEQ // WINDOW - 1)
    def _emit_conv_state():
        nc_ref[0] = gbuf_ref[CONV_HALO + blk - (CONV_WIDTH - 1):CONV_HALO + blk, :]

    gbuf_ref[0:CONV_HALO, :] = gbuf_ref[blk:blk + CONV_HALO, :]


def _even_core_prompt(z, sample_tile, sinks, qn, kn, glub, cw, cb, lng, lnb):
    nb = SEQ // WINDOW
    blk = WINDOW
    c, sp, sm = _rope_tables(jnp.arange(SEQ, dtype=jnp.int32))
    last = BATCH * nb - 1
    rowmap = lambda col: (lambda s: (jnp.minimum(s, last), col))
    per_seq = lambda s: (jnp.minimum(s, last) // nb, 0, 0)
    tab = pl.BlockSpec((blk, HEAD_DIM), lambda s: (s % nb, 0))
    vec = lambda n: pl.BlockSpec((1, n), lambda s: (0, 0))
    est = (2 * blk * (2048 + 1024 + 4 * 1024) * 4 + 4 * blk * 4096 * 2 + 4 * blk * 512 * 4
           + (blk + CONV_HALO) * CONV_CH * 4 + 3 * blk * CONV_CH * 4 + 2 * CONV_WIDTH * CONV_CH * 4 + (8 << 20))
    return pl.pallas_call(
        _even_prompt_body,
        out_shape=(jax.ShapeDtypeStruct((M_ROWS, 2 * ATT_WIDTH), BF16),
                   jax.ShapeDtypeStruct((BATCH, blk, KV_COLS), F32),
                   jax.ShapeDtypeStruct((BATCH, blk, KV_COLS), F32),
                   jax.ShapeDtypeStruct((BATCH, CONV_WIDTH - 1, CONV_CH), F32)),
        grid=(BATCH * nb + 1,),
        in_specs=[pl.BlockSpec(memory_space=pltpu.SMEM),
                  pl.BlockSpec((MERGE_TILE, 2 * ATT_WIDTH), lambda s: (0, 0)),
                  pl.BlockSpec((blk, 2048), rowmap(0)),
                  pl.BlockSpec((blk, 1024), rowmap(2)),
                  pl.BlockSpec((blk, 1024), rowmap(3)),
                  pl.BlockSpec((blk, 1024), rowmap(4)),
                  pl.BlockSpec((blk, 1024), rowmap(5)),
                  pl.BlockSpec((blk, 1024), rowmap(6)),
                  tab, tab, tab, vec(HEAD_DIM), vec(HEAD_DIM), vec(2 * CONV_CH),
                  pl.BlockSpec((CONV_WIDTH, CONV_CH), lambda s: (0, 0)),
                  vec(CONV_CH), vec(CONV_CH), vec(CONV_CH)],
        out_specs=(pl.BlockSpec((blk, 2 * ATT_WIDTH), lambda s: (s, 0)),
                   pl.BlockSpec((1, blk, KV_COLS), per_seq),
                   pl.BlockSpec((1, blk, KV_COLS), per_seq),
                   pl.BlockSpec((1, CONV_WIDTH - 1, CONV_CH), per_seq)),
        scratch_shapes=[pltpu.VMEM((blk, KV_COLS), BF16), pltpu.VMEM((blk, KV_COLS), BF16),
                        pltpu.VMEM((blk + CONV_HALO, CONV_CH), F32), pltpu.VMEM((blk, CONV_CH), F32),
                        pltpu.VMEM((N_KV_HEADS, Q_PER_KV * blk, HEAD_DIM), BF16),
                        pltpu.VMEM((N_KV_HEADS, 2 * blk, HEAD_DIM), BF16),
                        pltpu.VMEM((N_KV_HEADS, 2 * blk, HEAD_DIM), BF16),
                        pltpu.VMEM((blk, ATT_WIDTH), BF16),
                        pltpu.VMEM((CONV_HALO, CONV_CH), F32),
                        pltpu.VMEM((blk, EVEN_IN), F32), pltpu.VMEM((blk, KV_COLS), F32)],
        compiler_params=_cparams(1, est),
        name="even_core_prompt",
    )(sinks, sample_tile, z, z, z, z, z, z, c, sp, sm, qn.reshape(1, -1), kn.reshape(1, -1),
      glub.reshape(1, -1), cw, cb.reshape(1, -1), lng.reshape(1, -1), lnb.reshape(1, -1))


def _even_sample_body(sinks_ref, z_ref, ck_ref, cv_ref, sc_ref, c_ref, sp_ref, sm_ref, qn_ref, kn_ref,
                      glub_ref, cw_ref, cb_ref, lng_ref, lnb_ref,
                      mix_ref, nk_ref, nv_ref, nc_ref):
    zrow = lambda lo, hi: z_ref[0, :, lo:hi]
    c, sp, sm = c_ref[...], sp_ref[...], sm_ref[...]
    qn, kn = qn_ref[...], kn_ref[...]
    nbuf = WINDOW
    scale = HEAD_DIM ** -0.5
    qrow = lax.broadcasted_iota(jnp.int32, (V7X_SUBLANES, 1), 0)
    krow = lax.broadcasted_iota(jnp.int32, (nbuf, HEAD_DIM), 0)
    for g in range(N_KV_HEADS):
        ks = slice(g * HEAD_DIM, (g + 1) * HEAD_DIM)
        k_new = _norm_rope(zrow(Q_COLS + g * HEAD_DIM, Q_COLS + (g + 1) * HEAD_DIM), kn, c, sp, sm)
        v_new = zrow(Q_COLS + KV_COLS + g * HEAD_DIM, Q_COLS + KV_COLS + (g + 1) * HEAD_DIM)
        k_old = ck_ref[0, :, ks]
        v_old = cv_ref[0, :, ks]
        nk_ref[0, :, ks] = jnp.where(krow == nbuf - 1, k_new, pltpu.roll(k_old, nbuf - 1, 0))
        nv_ref[0, :, ks] = jnp.where(krow == nbuf - 1, v_new, pltpu.roll(v_old, nbuf - 1, 0))
        qs = [_norm_rope(zrow((g * Q_PER_KV + r) * HEAD_DIM, (g * Q_PER_KV + r + 1) * HEAD_DIM),
                         qn, c, sp, sm) for r in range(Q_PER_KV)]
        q8 = jnp.zeros((V7X_SUBLANES, HEAD_DIM), F32)
        for r in range(Q_PER_KV):
            q8 = jnp.where(qrow == r, qs[r], q8)
        q8b = q8.astype(BF16)
        s = lax.dot_general(q8b, k_old.astype(BF16), (((1,), (1,)), ((), ())),
                            preferred_element_type=F32) * scale
        k_new_r = k_new.astype(BF16).astype(F32)
        v_new_r = v_new.astype(BF16).astype(F32)
        s_self = jnp.sum(q8b.astype(F32) * k_new_r, axis=-1, keepdims=True) * scale
        sink = jnp.zeros((V7X_SUBLANES, 1), F32)
        for r in range(Q_PER_KV):
            sink = jnp.where(qrow == r, sinks_ref[g * Q_PER_KV + r], sink)
        m = jnp.maximum(jnp.maximum(jnp.max(s, axis=-1, keepdims=True), s_self), sink)
        p = jnp.exp(s - m)
        p_self = jnp.exp(s_self - m)
        denom = jnp.sum(p, axis=-1, keepdims=True) + p_self + jnp.exp(sink - m)
        o = jnp.dot((p / denom).astype(BF16), v_old.astype(BF16), preferred_element_type=F32)
        o = o + (p_self / denom).astype(BF16).astype(F32) * v_new_r
        for r in range(Q_PER_KV):
            h = g * Q_PER_KV + r
            mix_ref[0, :, h * HEAD_DIM:(h + 1) * HEAD_DIM] = o[r:r + 1, :]

    conv0 = Q_COLS + 2 * KV_COLS
    a = zrow(conv0, conv0 + CONV_CH) + glub_ref[:, 0:CONV_CH]
    gate = zrow(conv0 + CONV_CH, EVEN_IN) + glub_ref[:, CONV_CH:2 * CONV_CH]
    gl = a * _sigmoid(gate)
    nst = CONV_WIDTH - 1
    conv = jnp.sum(cw_ref[0:nst, :] * sc_ref[0], axis=0, keepdims=True) + cw_ref[nst:nst + 1, :] * gl
    y = _layer_norm_silu(conv + cb_ref[...], lng_ref[...], lnb_ref[...])
    mix_ref[0, :, ATT_WIDTH:ATT_WIDTH + CONV_CH] = y
    nc_ref[0, 0:nst - 1, :] = sc_ref[0, 1:nst, :]
    nc_ref[0, nst - 1:nst, :] = gl


def _even_core_sample(z, cache_k, cache_v, state_conv, sinks, qn, kn, glub, cw, cb, lng, lnb):
    nb = DEC_BATCH
    c, sp, sm = _rope_tables(jnp.full((1,), PAST_LEN, jnp.int32))
    vec = lambda n: pl.BlockSpec((1, n), lambda b: (0, 0))
    per_b = lambda r, n: pl.BlockSpec((1, r, n), lambda b: (b, 0, 0))
    nst = CONV_WIDTH - 1
    est = 2 * (EVEN_IN * 4 * nb + 4 * WINDOW * KV_COLS * 4 + 2 * 32 * CONV_CH * 4 + 4096 * 4 * 8) + (8 << 20)
    return pl.pallas_call(
        _even_sample_body,
        out_shape=(jax.ShapeDtypeStruct((nb, 1, 2 * ATT_WIDTH), F32),
                   jax.ShapeDtypeStruct((nb, WINDOW, KV_COLS), F32),
                   jax.ShapeDtypeStruct((nb, WINDOW, KV_COLS), F32),
                   jax.ShapeDtypeStruct((nb, nst, CONV_CH), F32)),
        grid=(nb,),
        in_specs=[pl.BlockSpec(memory_space=pltpu.SMEM),
                  per_b(1, EVEN_IN),
                  per_b(WINDOW, KV_COLS), per_b(WINDOW, KV_COLS), per_b(nst, CONV_CH),
                  vec(HEAD_DIM), vec(HEAD_DIM), vec(HEAD_DIM), vec(HEAD_DIM), vec(HEAD_DIM),
                  vec(2 * CONV_CH), pl.BlockSpec((CONV_WIDTH, CONV_CH), lambda b: (0, 0)),
                  vec(CONV_CH), vec(CONV_CH), vec(CONV_CH)],
        out_specs=(per_b(1, 2 * ATT_WIDTH), per_b(WINDOW, KV_COLS), per_b(WINDOW, KV_COLS),
                   per_b(nst, CONV_CH)),
        compiler_params=_cparams(1, est),
        name="even_core_sample",
    )(sinks, z[PROMPT_ROWS:PROMPT_ROWS + nb].reshape(nb, 1, EVEN_IN), cache_k, cache_v, state_conv, c, sp, sm,
      qn.reshape(1, -1), kn.reshape(1, -1), glub.reshape(1, -1), cw, cb.reshape(1, -1),
      lng.reshape(1, -1), lnb.reshape(1, -1))


def _sample_tile(sample_rows):
    pad = jnp.zeros((MERGE_TILE - DEC_BATCH, sample_rows.shape[1]), BF16)
    return jnp.concatenate([sample_rows.astype(BF16), pad], axis=0)


def _ssm_prep_body(lre_ref, lim_ref, ldt_ref, btre_ref, btim_ref,
                   are_ref, aim_ref, bbre_ref, bbim_ref, abre_ref, abim_ref, cre_ref, cim_ref):
    lre, lim = lre_ref[...], lim_ref[...]
    dt = jnp.exp(ldt_ref[...])
    mag = jnp.exp(lre * dt)
    ang = lim * dt
    are = mag * jnp.cos(ang)
    aim = mag * jnp.sin(ang)
    are_ref[...] = are
    aim_ref[...] = aim
    nre, nim = are - 1.0, aim
    den = lre * lre + lim * lim
    cre_ref[...] = (nre * lre + nim * lim) / den
    cim_ref[...] = (nim * lre - nre * lim) / den

    def per_group(g, carry):
        cr = cre_ref[pl.ds(g, 1), :]
        ci = cim_ref[pl.ds(g, 1), :]
        br, bi = btre_ref[g], btim_ref[g]
        bbr = cr * br - ci * bi
        bbi = cr * bi + ci * br
        bbre_ref[g] = bbr
        bbim_ref[g] = bbi
        ar = are_ref[pl.ds(g, 1), :]
        ai = aim_ref[pl.ds(g, 1), :]
        abre_ref[g] = ar * bbr - ai * bbi
        abim_ref[g] = ar * bbi + ai * bbr
        return carry

    lax.fori_loop(0, SSM_GROUPS, per_group, 0)


def _ssm_layout_body(bbre_ref, bbim_ref, abre_ref, abim_ref, cre_ref, cim_ref, b_ref, c_ref):
    n, ch, st = SSM_STATE, SSM_TILE_CH, SSM_TILE_ST
    p_shift = SSM_GROUP.bit_length() - 1
    n_shift = SSM_STATE.bit_length() - 1
    iota = lambda shape, d: lax.broadcasted_iota(jnp.int32, shape, d)
    spread = jnp.where((iota((n, st), 1) & (n - 1)) == iota((n, st), 0), 1.0, 0.0).astype(BF16)
    own_b = (iota((ch, st), 0) >> p_shift) == (iota((ch, st), 1) >> n_shift)
    for k, src_ref in enumerate((bbre_ref, bbim_ref, abre_ref, abim_ref)):
        rep = jnp.dot(src_ref[...].astype(BF16), spread, preferred_element_type=F32)
        r0, c0 = (k // 2) * ch, (k % 2) * st
        b_ref[0, r0:r0 + ch, c0:c0 + st] = jnp.where(own_b, rep, 0.0).astype(BF16)
    spread_t = jnp.where((iota((st, n), 0) & (n - 1)) == iota((st, n), 1), 1.0, 0.0).astype(BF16)
    own_c = (iota((st, ch), 0) >> n_shift) == (iota((st, ch), 1) >> p_shift)
    for k, (src_ref, sign) in enumerate(((cre_ref, 1.0), (cim_ref, -1.0))):
        rep = lax.dot_general(spread_t, src_ref[...].astype(BF16), (((1,), (1,)), ((), ())),
                              preferred_element_type=F32)
        c_ref[0, k * st:(k + 1) * st, :] = jnp.where(own_c, sign * rep, 0.0).astype(BF16)


def _ssm_params(lam_re, lam_im, log_dt, b_re, b_im, c_re, c_im, d_skip):
    g, n, p = SSM_GROUPS, SSM_STATE, SSM_GROUP
    bt_re = jnp.swapaxes(b_re, 1, 2)
    bt_im = jnp.swapaxes(b_im, 1, 2)
    gn = jax.ShapeDtypeStruct((g, n), F32)
    gpn = jax.ShapeDtypeStruct((g, p, n), F32)
    are, aim, bbre, bbim, abre, abim = pl.pallas_call(
        _ssm_prep_body,
        out_shape=(gn, gn, gpn, gpn, gpn, gpn),
        scratch_shapes=[pltpu.VMEM((g, n), F32), pltpu.VMEM((g, n), F32)],
        name="ssm_prep",
    )(lam_re, lam_im, log_dt.reshape(g, 1), bt_re, bt_im)

    t, tg = SSM_TILES, SSM_TILE_GROUPS
    ch, st = SSM_TILE_CH, SSM_TILE_ST
    rows_gp = lambda m: m.reshape(g * p, n)
    src_spec = pl.BlockSpec((ch, n), lambda i: (i, 0))
    b_mat, c_mat = pl.pallas_call(
        _ssm_layout_body,
        out_shape=(jax.ShapeDtypeStruct((t, 2 * ch, 2 * st), BF16),
                   jax.ShapeDtypeStruct((t, 2 * st, ch), BF16)),
        grid=(t,),
        in_specs=[src_spec] * 6,
        out_specs=(pl.BlockSpec((1, 2 * ch, 2 * st), lambda i: (i, 0, 0)),
                   pl.BlockSpec((1, 2 * st, ch), lambda i: (i, 0, 0))),
        compiler_params=_cparams(1, 2 * (2 * ch * 2 * st * 2 + 2 * st * ch * 2) + 8 * ch * st * 4),
        name="ssm_layout",
    )(rows_gp(bbre), rows_gp(bbim), rows_gp(abre), rows_gp(abim), rows_gp(c_re), rows_gp(c_im))
    return (b_mat, c_mat, are.reshape(t, 1, tg * n), aim.reshape(t, 1, tg * n),
            d_skip.reshape(t, 1, tg * p))


def _gelu_tanh(y):
    return 0.5 * y * (1.0 + jnp.tanh(math.sqrt(2.0 / math.pi) * (y + 0.044715 * (y * y * y))))


def _ssm_prompt_body(u0_ref, u1_ref, u2_ref, u3_ref, tile_ref, b_ref, c_ref, are_ref, aim_ref, d_ref,
                     y_ref, hre_ref, him_ref,
                     us_ref, ys_ref, bu_ref, h_ref, carry_ref, p_re_ref, p_im_ref):
    st = SSM_TILE_ST
    sub = V7X_SUBLANES
    nl = st // V7X_LANES
    nh = SSM_TILE_CH // V7X_LANES
    tt = pl.program_id(1)
    steps = u0_ref.shape[0]
    rows = steps * BATCH

    @pl.when(tt == 0)
    def _init():
        carry_ref[...] = jnp.zeros_like(carry_ref)
        ar = jnp.broadcast_to(are_ref[0], (sub, st))
        ai = jnp.broadcast_to(aim_ref[0], (sub, st))
        lo = lax.broadcasted_iota(jnp.int32, (sub, st), 0) < BATCH
        p_re_ref[...] = jnp.where(lo, ar, ar * ar - ai * ai)
        p_im_ref[...] = jnp.where(lo, ai, 2.0 * ar * ai)

    for b, ub_ref in enumerate((u0_ref, u1_ref, u2_ref, u3_ref)):
        for hh in range(nh):
            us_ref[hh, pl.ds(b, steps, stride=BATCH), :] = ub_ref[:, hh * V7X_LANES:(hh + 1) * V7X_LANES]
    u = jnp.concatenate([us_ref[hh] for hh in range(nh)], axis=1)
    second = (lax.broadcasted_iota(jnp.int32, u.shape, 0) % sub) >= BATCH
    u_prev = jnp.where(second, pltpu.roll(u, BATCH, 0), 0.0)
    u2 = jnp.concatenate([u, u_prev], axis=1).astype(BF16)
    half = rows // SSM_PARTS
    for hf in range(SSM_PARTS):
        bu_ref[hf] = jnp.dot(u2[hf * half:(hf + 1) * half], b_ref[0], preferred_element_type=F32)

    lo8 = lax.broadcasted_iota(jnp.int32, (sub, V7X_LANES), 0) < BATCH
    pack = 2 * sub

    def scan_half(hf, carry):
        for i in range(half // pack):
            r0 = i * pack
            new = []
            for j in range(nl):
                lre = slice(j * V7X_LANES, (j + 1) * V7X_LANES)
                lim = slice(st + j * V7X_LANES, st + (j + 1) * V7X_LANES)
                pr, pi = p_re_ref[:, lre], p_im_ref[:, lre]
                hr, hi = carry[2 * j], carry[2 * j + 1]
                outs_r, outs_i = [], []
                for k in range(2):
                    yr = bu_ref[hf, r0 + k * sub:r0 + (k + 1) * sub, lre]
                    yi = bu_ref[hf, r0 + k * sub:r0 + (k + 1) * sub, lim]
                    hbr = jnp.where(lo8, pltpu.roll(hr, BATCH, 0), hr)
                    hbi = jnp.where(lo8, pltpu.roll(hi, BATCH, 0), hi)
                    hr = yr + pr * hbr - pi * hbi
                    hi = yi + pr * hbi + pi * hbr
                    outs_r.append(hr)
                    outs_i.append(hi)
                h_ref[hf, r0:r0 + pack, lre] = jnp.concatenate(outs_r, axis=0).astype(h_ref.dtype)
                h_ref[hf, r0:r0 + pack, lim] = jnp.concatenate(outs_i, axis=0).astype(h_ref.dtype)
                new += [hr, hi]
            carry = tuple(new)
        return carry

    carry = []
    for j in range(nl):
        carry += [carry_ref[:, j * V7X_LANES:(j + 1) * V7X_LANES],
                  carry_ref[:, st + j * V7X_LANES:st + (j + 1) * V7X_LANES]]
    carry = tuple(carry)
    ys = []
    for hf in range(SSM_PARTS):
        carry = scan_half(hf, carry)
        ys.append(jnp.dot(h_ref[hf], c_ref[0], preferred_element_type=F32))
    for j in range(nl):
        carry_ref[:, j * V7X_LANES:(j + 1) * V7X_LANES] = carry[2 * j]
        carry_ref[:, st + j * V7X_LANES:st + (j + 1) * V7X_LANES] = carry[2 * j + 1]

    y = jnp.concatenate(ys, axis=0) + d_ref[0] * u
    y = _gelu_tanh(y)
    for hh in range(nh):
        ys_ref[hh] = y[:, hh * V7X_LANES:(hh + 1) * V7X_LANES]
    for b in range(BATCH):
        for hh in range(nh):
            y_ref[b, :, hh * V7X_LANES:(hh + 1) * V7X_LANES] = (
                ys_ref[hh, pl.ds(b, steps, stride=BATCH), :].astype(y_ref.dtype))
    y_ref[BATCH] = jnp.zeros((steps, SSM_TILE_CH), y_ref.dtype)

    @pl.when(tt == 0)
    def _sample_tile():
        y_ref[BATCH, 0:MERGE_TILE, :] = tile_ref[...]

    hre_ref[...] = carry_ref[:, 0:st]
    him_ref[...] = carry_ref[:, st:2 * st]


def _ssm_prompt(u, sample_tile, b_mat, c_mat, are, aim, d):
    steps = SSM_STEPS
    rows = steps * BATCH
    nt = SEQ // steps
    st, ch = SSM_TILE_ST, SSM_TILE_CH
    nh = ch // V7X_LANES
    est = (8 * steps * ch * 4 + 2 * BATCH * steps * ch * 2 + 6 * ch * 2 * st * 2 + 2 * rows * 2 * st * 4
           + rows * 2 * st * 2 + 4 * rows * ch * 4 + 8 * 8 * st * 4 + (4 << 20))
    u_spec = lambda b: pl.BlockSpec((steps, ch), lambda g, t, b=b: (b * nt + t, g))
    return pl.pallas_call(
        _ssm_prompt_body,
        out_shape=(jax.ShapeDtypeStruct((BATCH + 1, SEQ, D_MODEL), BF16),
                   jax.ShapeDtypeStruct((V7X_SUBLANES, SSM_GROUPS * SSM_STATE), F32),
                   jax.ShapeDtypeStruct((V7X_SUBLANES, SSM_GROUPS * SSM_STATE), F32)),
        grid=(SSM_TILES, nt),
        in_specs=[u_spec(0), u_spec(1), u_spec(2), u_spec(3),
                  pl.BlockSpec((MERGE_TILE, ch), lambda g, t: (0, g)),
                  pl.BlockSpec((1, 2 * ch, 2 * st), lambda g, t: (g, 0, 0)),
                  pl.BlockSpec((1, 2 * st, ch), lambda g, t: (g, 0, 0)),
                  pl.BlockSpec((1, 1, st), lambda g, t: (g, 0, 0)),
                  pl.BlockSpec((1, 1, st), lambda g, t: (g, 0, 0)),
                  pl.BlockSpec((1, 1, ch), lambda g, t: (g, 0, 0))],
        out_specs=(pl.BlockSpec((BATCH + 1, steps, ch), lambda g, t: (0, t, g)),
                   pl.BlockSpec((V7X_SUBLANES, st), lambda g, t: (0, g)),
                   pl.BlockSpec((V7X_SUBLANES, st), lambda g, t: (0, g))),
        scratch_shapes=[pltpu.VMEM((nh, rows, V7X_LANES), F32), pltpu.VMEM((nh, rows, V7X_LANES), F32),
                        pltpu.VMEM((SSM_PARTS, rows // SSM_PARTS, 2 * st), F32),
                        pltpu.VMEM((SSM_PARTS, rows // SSM_PARTS, 2 * st), BF16),
                        pltpu.VMEM((V7X_SUBLANES, 2 * st), F32),
                        pltpu.VMEM((V7X_SUBLANES, st), F32), pltpu.VMEM((V7X_SUBLANES, st), F32)],
        compiler_params=_cparams(2, est),
        name="ssm_prompt",
    )(u, u, u, u, sample_tile, b_mat, c_mat, are, aim, d)


def _ssm_sample_body(u_ref, h0re_ref, h0im_ref, b_ref, c_ref, are_ref, aim_ref, d_ref,
                     y_ref, hre_ref, him_ref):
    st = SSM_TILE_ST
    u = u_ref[...]
    bu = jnp.dot(u.astype(BF16), b_ref[0], preferred_element_type=F32)
    ar, ai = are_ref[0], aim_ref[0]
    h0r, h0i = h0re_ref[...], h0im_ref[...]
    hr = ar * h0r - ai * h0i + bu[:, 0:st]
    hi = ar * h0i + ai * h0r + bu[:, st:2 * st]
    hre_ref[...] = hr
    him_ref[...] = hi
    hcat = jnp.concatenate([hr, hi], axis=1).astype(BF16)
    y = jnp.dot(hcat, c_ref[0], preferred_element_type=F32) + d_ref[0] * u
    y_ref[...] = _gelu_tanh(y).astype(y_ref.dtype)


def _ssm_sample(u, h0_re, h0_im, b_mat, c_mat, are, aim, d):
    nb = DEC_BATCH
    st, ch = SSM_TILE_ST, SSM_TILE_CH
    est = 2 * (nb * ch * 6 + 4 * nb * st * 4 + 2 * ch * 2 * st * 2) + (4 << 20)
    return pl.pallas_call(
        _ssm_sample_body,
        out_shape=(jax.ShapeDtypeStruct((nb, D_MODEL), BF16),
                   jax.ShapeDtypeStruct((nb, SSM_GROUPS * SSM_STATE), F32),
                   jax.ShapeDtypeStruct((nb, SSM_GROUPS * SSM_STATE), F32)),
        grid=(SSM_TILES,),
        in_specs=[pl.BlockSpec((nb, ch), lambda g: (PROMPT_ROWS // DEC_BATCH, g)),
                  pl.BlockSpec((nb, st), lambda g: (0, g)),
                  pl.BlockSpec((nb, st), lambda g: (0, g)),
                  pl.BlockSpec((1, ch, 2 * st), lambda g: (g, 0, 0)),
                  pl.BlockSpec((1, 2 * st, ch), lambda g: (g, 0, 0)),
                  pl.BlockSpec((1, 1, st), lambda g: (g, 0, 0)),
                  pl.BlockSpec((1, 1, st), lambda g: (g, 0, 0)),
                  pl.BlockSpec((1, 1, ch), lambda g: (g, 0, 0))],
        out_specs=(pl.BlockSpec((nb, ch), lambda g: (0, g)),
                   pl.BlockSpec((nb, st), lambda g: (0, g)),
                   pl.BlockSpec((nb, st), lambda g: (0, g))),
        compiler_params=_cparams(1, est),
        name="ssm_sample",
    )(u, h0_re, h0_im, b_mat, c_mat, are, aim, d)


def kernel(x_prompt, x_sample, cache_swa_k, cache_swa_v, state_conv, state_ssm_re, state_ssm_im, ffn_norm, ffn_w_gu, ffn_w_down, mix_norm, even_w_in, even_q_norm, even_k_norm, even_sinks, even_glu_b, even_conv_w, even_conv_b, even_ln_g, even_ln_b, even_w_out, odd_w_in, odd_lam_re, odd_lam_im, odd_log_dt, odd_b_re, odd_b_im, odd_c_re, odd_c_im, odd_d, odd_w_gate, odd_w_out):
    nb = DEC_BATCH
    ssm = _ssm_params(odd_lam_re[0], odd_lam_im[0], odd_log_dt[0], odd_b_re[0], odd_b_im[0],
                      odd_c_re[0], odd_c_im[0], odd_d[0])

    x, h = _merge_rmsnorm(x_prompt.reshape(PROMPT_ROWS, D_MODEL), x_sample.reshape(nb, D_MODEL),
                          ffn_norm[0, 0])
    x, gu_next = _half_ffn(x, h, ffn_w_gu, ffn_w_down, (0, 0), next_gu=(0, 1))

    even_p = (even_sinks[0], even_q_norm[0], even_k_norm[0], even_glu_b[0], even_conv_w[0],
              even_conv_b[0], even_ln_g[0], even_ln_b[0])
    z = _matmul(_rmsnorm(x, mix_norm[0]), even_w_in, (0,), n_cols=EVEN_IN, name="even_in")
    mix_s, sk, sv, sc = _even_core_sample(
        z, cache_swa_k.reshape(nb, WINDOW, KV_COLS), cache_swa_v.reshape(nb, WINDOW, KV_COLS),
        state_conv.reshape(nb, CONV_WIDTH - 1, CONV_CH), *even_p)
    mix, pk, pv, pc = _even_core_prompt(z, _sample_tile(mix_s.reshape(nb, D_MODEL)), *even_p)
    x = _matmul(mix, even_w_out, (0,), n_cols=D_MODEL, epilogue="res", extra=x, name="even_out")
    x, gu_next = _half_ffn(x, _rmsnorm(x, ffn_norm[0, 1]), ffn_w_gu, ffn_w_down, (0, 1),
                           gu_bf16=gu_next, next_gu=(1, 0))

    x, gu_next = _half_ffn(x, _rmsnorm(x, ffn_norm[1, 0]), ffn_w_gu, ffn_w_down, (1, 0),
                           gu_bf16=gu_next, next_gu=(1, 1))
    u = _matmul(_rmsnorm(x, mix_norm[1]), odd_w_in, (0,), n_cols=D_MODEL, name="odd_in")
    y_s, sre, sim = _ssm_sample(u, state_ssm_re.reshape(nb, -1), state_ssm_im.reshape(nb, -1), *ssm)
    y3, pre8, pim8 = _ssm_prompt(u, _sample_tile(y_s), *ssm)
    y = y3.reshape((BATCH + 1) * SEQ, D_MODEL)
    yg = _matmul(y, odd_w_gate, (0,), n_cols=D_MODEL, epilogue="gate", out_dtype=BF16, extra=y,
                 name="odd_gate")
    x = _matmul(yg, odd_w_out, (0,), n_cols=D_MODEL, epilogue="res", extra=x, name="odd_out")
    (y_p, y_smp), _ = _half_ffn(x, _rmsnorm(x, ffn_norm[1, 1]), ffn_w_gu, ffn_w_down, (1, 1),
                                gu_bf16=gu_next, split_out=True)

    kv5 = lambda t, b: t.reshape(1, b, WINDOW, N_KV_HEADS, HEAD_DIM)
    st4 = lambda t, b: t.reshape(1, b, SSM_GROUPS, SSM_STATE)
    return (y_p.reshape(BATCH, SEQ, D_MODEL), y_smp.reshape(nb, 1, D_MODEL),
            kv5(pk, BATCH), kv5(pv, BATCH), pc.reshape(1, BATCH, CONV_WIDTH - 1, CONV_CH),
            st4(pre8[BATCH:2 * BATCH], BATCH), st4(pim8[BATCH:2 * BATCH], BATCH),
            kv5(sk, nb), kv5(sv, nb), sc.reshape(1, nb, CONV_WIDTH - 1, CONV_CH),
            st4(sre, nb), st4(sim, nb))
```
